```python
import jax
import jax.numpy as jnp
from jax import lax
import numpy as np

D_MODEL = 2048
BATCH = 4
SEQ = 4096
DEPTH = 1

CONV_DIM = D_MODEL
CONV_WIDTH = 3
HEAD_DIM = 128
N_HEADS = D_MODEL // HEAD_DIM
N_KV_HEADS = 4
ROT_DIM = HEAD_DIM // 4
IDX_HEADS = 16
IDX_DIM = 64
IDX_ROT_DIM = IDX_DIM // 4
TOPK_MAX = 256
Q_BLOCK = 128
ROPE_THETA = 500000.0
MEM_LEN = 256
X_HEADS = 4
X_HEAD_DIM = 128
N_EXPERTS = 32
TOP_K_EXPERTS = 4
D_FF = D_MODEL
SWIGLU_LIMIT = 7.0
SWIGLU_ALPHA = 1.702
EXPERT_BLOCK = 256
NORM_EPS = 1e-6
MAX_POS_OFFSET = 1024

IN_SPLITS = (CONV_DIM, CONV_DIM, CONV_DIM, N_HEADS * HEAD_DIM, N_KV_HEADS * HEAD_DIM, N_KV_HEADS * HEAD_DIM, IDX_HEADS * IDX_DIM, IDX_DIM, IDX_HEADS, CONV_DIM, N_HEADS * HEAD_DIM)
D_IN = sum(IN_SPLITS)

kernel_name = 'hybrid_conv_dsa_memxattn_moe_block'


def rms_norm(x, g):
    x32 = x.astype(jnp.float32)
    y = x32 * lax.rsqrt(jnp.mean(x32 * x32, axis=-1, keepdims=True) + NORM_EPS)
    return (y * g.astype(jnp.float32)).astype(x.dtype)


def split_cols(z, sizes):
    outs, off = [], 0
    for n in sizes:
        outs.append(z[..., off:off + n])
        off += n
    return outs


def rope_partial(x, positions, rot_dim):
    half = rot_dim // 2
    inv_freq = jnp.float32(ROPE_THETA) ** (-(jnp.arange(half, dtype=jnp.float32) * 2.0 / rot_dim))
    ang = positions.astype(jnp.float32)[..., None] * inv_freq
    cos = jnp.cos(ang)[:, :, None, :]
    sin = jnp.sin(ang)[:, :, None, :]
    xr = x[..., :rot_dim].astype(jnp.float32)
    x1, x2 = xr[..., :half], xr[..., half:]
    rot = jnp.concatenate([x1 * cos - x2 * sin, x2 * cos + x1 * sin], axis=-1).astype(x.dtype)
    return jnp.concatenate([rot, x[..., rot_dim:]], axis=-1)


def causal_depthwise_conv(u, w):
    c = u.shape[-1]
    return lax.conv_general_dilated(
        u, w[:, None, :].astype(u.dtype), window_strides=(1,),
        padding=[(CONV_WIDTH - 1, 0)], dimension_numbers=('NWC', 'WIO', 'NWC'),
        feature_group_count=c)


def dsa_attention(q, k, v, q_idx, k_idx, w_idx):
    b, s = q.shape[0], q.shape[1]
    topk = min(TOPK_MAX, s // 4)
    nb = s // Q_BLOCK
    grp = N_HEADS // N_KV_HEADS
    key_pos = jnp.arange(s)
    scale = HEAD_DIM ** -0.5

    def to_blocks(t):
        return t.reshape((b, nb, Q_BLOCK) + t.shape[2:]).swapaxes(0, 1)

    def block(args):
        qb, qib, wb, start = args
        qpos = start + jnp.arange(Q_BLOCK)
        causal = key_pos[None, :] <= qpos[:, None]
        logits = jnp.einsum('bqhd,bsd->bqhs', qib, k_idx, preferred_element_type=jnp.float32)
        score = jnp.einsum('bqhs,bqh->bqs', jax.nn.relu(logits), wb.astype(jnp.float32))
        score = jnp.where(causal[None], score, -jnp.inf)
        _, sel = lax.top_k(score, topk)
        kg = jax.vmap(lambda kk, ii: kk[ii])(k, sel)
        vg = jax.vmap(lambda vv, ii: vv[ii])(v, sel)
        qg = qb.reshape(b, Q_BLOCK, N_KV_HEADS, grp, HEAD_DIM)
        att = jnp.einsum('bqgnd,bqkgd->bqgnk', qg, kg, preferred_element_type=jnp.float32) * scale
        valid = (sel <= qpos[None, :, None])[:, :, None, None, :]
        p = jax.nn.softmax(jnp.where(valid, att, -jnp.inf), axis=-1)
        o = jnp.einsum('bqgnk,bqkgd->bqgnd', p.astype(vg.dtype), vg)
        return o.reshape(b, Q_BLOCK, N_HEADS * HEAD_DIM)

    starts = jnp.arange(nb) * Q_BLOCK
    out = lax.map(block, (to_blocks(q), to_blocks(q_idx), to_blocks(w_idx), starts))
    return out.swapaxes(0, 1).reshape(b, s, N_HEADS * HEAD_DIM)


def memory_cross_attention(h, mem_n, wq, wk, wv, wo):
    b, s = h.shape[0], h.shape[1]
    m = mem_n.shape[1]
    q = (h @ wq).reshape(b, s, X_HEADS, X_HEAD_DIM)
    k = (mem_n @ wk).reshape(b, m, X_HEADS, X_HEAD_DIM)
    v = (mem_n @ wv).reshape(b, m, X_HEADS, X_HEAD_DIM)
    att = jnp.einsum('bshd,bmhd->bhsm', q, k, preferred_element_type=jnp.float32) * (X_HEAD_DIM ** -0.5)
    p = jax.nn.softmax(att, axis=-1)
    o = jnp.einsum('bhsm,bmhd->bshd', p.astype(v.dtype), v).reshape(b, s, X_HEADS * X_HEAD_DIM)
    return o @ wo


def moe_ffn(xf, w_router, b_router, w_gu, b_gu, w_dn, b_dn):
    t, d = xf.shape
    logits = (xf @ w_router + b_router).astype(jnp.float32)
    top_val, top_idx = lax.top_k(logits, TOP_K_EXPERTS)
    gates = jax.nn.softmax(top_val, axis=-1)
    n_slots = t * TOP_K_EXPERTS
    flat_e = top_idx.reshape(-1)
    flat_tok = jnp.arange(n_slots, dtype=jnp.int32) // TOP_K_EXPERTS
    flat_g = gates.reshape(-1)
    order = jnp.argsort(flat_e)
    sorted_e = flat_e[order]
    counts = jnp.bincount(flat_e, length=N_EXPERTS)
    padded = (counts + EXPERT_BLOCK - 1) // EXPERT_BLOCK * EXPERT_BLOCK
    start = jnp.cumsum(counts) - counts
    pstart = jnp.cumsum(padded) - padded
    dest = pstart[sorted_e] + jnp.arange(n_slots) - start[sorted_e]
    n_blocks = -(-n_slots // EXPERT_BLOCK) + N_EXPERTS
    cap = n_blocks * EXPERT_BLOCK
    buf_tok = jnp.full((cap,), t, dtype=jnp.int32).at[dest].set(flat_tok[order])
    buf_g = jnp.zeros((cap,), jnp.float32).at[dest].set(flat_g[order])
    block_e = jnp.minimum(jnp.searchsorted(jnp.cumsum(padded), jnp.arange(n_blocks) * EXPERT_BLOCK, side='right'), N_EXPERTS - 1)
    x_pad = jnp.concatenate([xf, jnp.zeros((1, d), xf.dtype)], axis=0)

    def expert_block(args):
        tok, e = args
        xb = x_pad[tok]
        gu = xb @ w_gu[e] + b_gu[e]
        gate, up = gu[:, :D_FF], gu[:, D_FF:]
        gate = jnp.minimum(gate, SWIGLU_LIMIT)
        up = jnp.clip(up, -SWIGLU_LIMIT, SWIGLU_LIMIT)
        hdn = (up + 1.0) * (gate * jax.nn.sigmoid(SWIGLU_ALPHA * gate))
        return hdn @ w_dn[e] + b_dn[e]

    y = lax.map(expert_block, (buf_tok.reshape(n_blocks, EXPERT_BLOCK), block_e))
    y = y.reshape(cap, d) * buf_g[:, None].astype(y.dtype)
    return jax.ops.segment_sum(y, buf_tok, num_segments=t + 1)[:t]


def setup_inputs(seed: int = 0) -> dict:
    key = jax.random.key(seed)
    ks = jax.random.split(key, 24)
    f32 = jnp.float32
    L = DEPTH

    def nrm(k, shape, scale):
        return jax.random.normal(k, shape, f32) * scale

    def gain(k, shape):
        return 1.0 + 0.02 * jax.random.normal(k, shape, f32)

    x = nrm(ks[0], (BATCH, SEQ, D_MODEL), 1.0)
    mem = nrm(ks[1], (BATCH, MEM_LEN, D_MODEL), 1.0)
    positions = (jax.random.randint(ks[2], (BATCH, 1), 0, MAX_POS_OFFSET, dtype=jnp.int32)
                 + jnp.arange(SEQ, dtype=jnp.int32)[None, :]).astype(jnp.int32)
    return {
        'x': x,
        'mem': mem,
        'positions': positions,
        'norm_mix': gain(ks[3], (L, D_MODEL)),
        'w_in': nrm(ks[4], (L, D_MODEL, D_IN), D_MODEL ** -0.5),
        'conv_w': nrm(ks[5], (L, CONV_WIDTH, CONV_DIM), CONV_WIDTH ** -0.5),
        'w_out': nrm(ks[6], (L, D_MODEL, D_MODEL), D_MODEL ** -0.5),
        'norm_xattn': gain(ks[7], (L, D_MODEL)),
        'norm_mem': gain(ks[8], (L, D_MODEL)),
        'wq_x': nrm(ks[9], (L, D_MODEL, X_HEADS * X_HEAD_DIM), D_MODEL ** -0.5),
        'wk_x': nrm(ks[10], (L, D_MODEL, X_HEADS * X_HEAD_DIM), D_MODEL ** -0.5),
        'wv_x': nrm(ks[11], (L, D_MODEL, X_HEADS * X_HEAD_DIM), D_MODEL ** -0.5),
        'wo_x': nrm(ks[12], (L, X_HEADS * X_HEAD_DIM, D_MODEL), (X_HEADS * X_HEAD_DIM) ** -0.5),
        'norm_ffn': gain(ks[13], (L, D_MODEL)),
        'w_router': nrm(ks[14], (L, D_MODEL, N_EXPERTS), D_MODEL ** -0.5),
        'b_router': nrm(ks[15], (L, N_EXPERTS), 0.01),
        'w_gate_up': nrm(ks[16], (L, N_EXPERTS, D_MODEL, 2 * D_FF), D_MODEL ** -0.5),
        'b_gate_up': nrm(ks[17], (L, N_EXPERTS, 2 * D_FF), 0.01),
        'w_down': nrm(ks[18], (L, N_EXPERTS, D_FF, D_MODEL), D_FF ** -0.5),
        'b_down': nrm(ks[19], (L, N_EXPERTS, D_MODEL), 0.01),
        'norm_final': gain(ks[20], (D_MODEL,)),
    }


def reference(x, mem, positions, norm_mix, w_in, conv_w, w_out, norm_xattn, norm_mem, wq_x, wk_x, wv_x, wo_x, norm_ffn, w_router, b_router, w_gate_up, b_gate_up, w_down, b_down, norm_final):
    b, s, d = x.shape
    h = x
    for l in range(DEPTH):
        u = rms_norm(h, norm_mix[l])
        z = u @ w_in[l]
        (cb, cc, cx, q, k, v, qi, ki, wi, g_conv, g_attn) = split_cols(z, IN_SPLITS)
        y_conv = cb * causal_depthwise_conv(cc * cx, conv_w[l])
        q = rope_partial(q.reshape(b, s, N_HEADS, HEAD_DIM), positions, ROT_DIM)
        k = rope_partial(k.reshape(b, s, N_KV_HEADS, HEAD_DIM), positions, ROT_DIM)
        v = v.reshape(b, s, N_KV_HEADS, HEAD_DIM)
        qi = rope_partial(qi.reshape(b, s, IDX_HEADS, IDX_DIM), positions, IDX_ROT_DIM)
        ki = rope_partial(ki[:, :, None, :], positions, IDX_ROT_DIM)[:, :, 0, :]
        wi = wi * (IDX_HEADS ** -0.5 * IDX_DIM ** -0.5)
        y_attn = dsa_attention(q, k, v, qi, ki, wi)
        merged = jax.nn.sigmoid(g_conv) * y_conv + jax.nn.sigmoid(g_attn) * y_attn
        h = h + merged @ w_out[l]
        h = h + memory_cross_attention(rms_norm(h, norm_xattn[l]), rms_norm(mem, norm_mem[l]),
                                       wq_x[l], wk_x[l], wv_x[l], wo_x[l])
        hf = rms_norm(h, norm_ffn[l]).reshape(b * s, d)
        h = h + moe_ffn(hf, w_router[l], b_router[l], w_gate_up[l], b_gate_up[l],
                        w_down[l], b_down[l]).reshape(b, s, d)
    return rms_norm(h, norm_final)
```

```python
import functools

import jax
import jax.numpy as jnp
from jax import lax
from jax.experimental import pallas as pl
from jax.experimental.pallas import tpu as pltpu

F32 = jnp.float32
BF16 = jnp.bfloat16
I32 = jnp.int32

HEAD_DIM = 128
N_HEADS = 16
N_KV_HEADS = 4
GROUP = N_HEADS // N_KV_HEADS
ROT_DIM = 32
IDX_HEADS = 16
IDX_DIM = 64
IDX_ROT_DIM = 16
TOPK_MAX = 256
Q_BLOCK = 128
ROPE_THETA = 500000.0
X_HEADS = 4
X_HEAD_DIM = 128
N_EXPERTS = 32
TOP_K_EXPERTS = 4
SWIGLU_LIMIT = 7.0
SWIGLU_ALPHA = 1.702
EXPERT_BLOCK = 256
NORM_EPS = 1e-6
CONV_WIDTH = 3

LANES = 128
NEG = -1e30
INT_MIN = -2 ** 31
MIB = 1024 * 1024


def _params(sem, vmem_mib):
    return pltpu.CompilerParams(dimension_semantics=sem, vmem_limit_bytes=vmem_mib * MIB)


def _rms(x, g):
    ms = jnp.mean(x * x, axis=-1, keepdims=True)
    return x * lax.rsqrt(ms + NORM_EPS) * g


def _norm_matmul_body(x_ref, g_ref, w_ref, o_ref, a_scr):
    @pl.when(pl.program_id(1) == 0)
    def _():
        a_scr[...] = _rms(x_ref[...], g_ref[...]).astype(BF16)

    o_ref[...] = jnp.dot(a_scr[...], w_ref[...], preferred_element_type=F32).astype(o_ref.dtype)


def norm_matmul(x, g, w, out_dtype, tm, tn):
    m, k = x.shape
    n = w.shape[1]
    tm = min(tm, m)
    assert m % tm == 0 and n % tn == 0
    return pl.pallas_call(
        _norm_matmul_body,
        grid=(m // tm, n // tn),
        in_specs=[pl.BlockSpec((tm, k), lambda i, j: (i, 0)),
                  pl.BlockSpec((1, k), lambda i, j: (0, 0)),
                  pl.BlockSpec((k, tn), lambda i, j: (0, j))],
        out_specs=pl.BlockSpec((tm, tn), lambda i, j: (i, j)),
        out_shape=jax.ShapeDtypeStruct((m, n), out_dtype),
        scratch_shapes=[pltpu.VMEM((tm, k), BF16)],
        compiler_params=_params(("parallel", "arbitrary"), 48),
        name="norm_matmul",
    )(x, g.reshape(1, k), w)


def _rope(x, tab, shift):
    c, s1, s2 = tab[:, 0:LANES], tab[:, LANES:2 * LANES], tab[:, 2 * LANES:3 * LANES]
    return x * c + pltpu.roll(x, shift, 1) * s1 + pltpu.roll(x, LANES - shift, 1) * s2


def _rope_body(q_ref, qi_ref, k_ref, kw_ref, tq_ref, ti_ref, tk_ref,
               qo_ref, qio_ref, ko_ref, kio_ref, wio_ref, *, tm):
    tq = tq_ref[...]
    scale = HEAD_DIM ** -0.5
    for h in range(N_HEADS):
        xh = q_ref[:, h * LANES:(h + 1) * LANES]
        qo_ref[0, 0, h * tm:(h + 1) * tm, :] = (_rope(xh, tq, ROT_DIM // 2) * scale).astype(BF16)
    for g in range(N_KV_HEADS):
        xh = k_ref[:, g * LANES:(g + 1) * LANES]
        ko_ref[:, g * LANES:(g + 1) * LANES] = _rope(xh, tq, ROT_DIM // 2).astype(BF16)
    ti = ti_ref[...]
    for j in range(IDX_HEADS * IDX_DIM // LANES):
        xr = _rope(qi_ref[:, j * LANES:(j + 1) * LANES], ti, IDX_ROT_DIM // 2).astype(BF16)
        for half in range(LANES // IDX_DIM):
            h = j * (LANES // IDX_DIM) + half
            part = xr[:, half * IDX_DIM:(half + 1) * IDX_DIM]
            for qb in range(tm // Q_BLOCK):
                qio_ref[0, qb, h * Q_BLOCK:(h + 1) * Q_BLOCK, :] = part[qb * Q_BLOCK:(qb + 1) * Q_BLOCK]
    kw = kw_ref[...]
    kr = _rope(kw, tk_ref[...], IDX_ROT_DIM // 2)
    kio_ref[...] = kr[:, 0:IDX_DIM].astype(BF16)
    wio_ref[...] = kw[:, IDX_DIM:IDX_DIM + IDX_HEADS] * (IDX_HEADS ** -0.5 * IDX_DIM ** -0.5)


def rope_split(zb, tq, ti, tk, b, s, tm):
    t = b * s
    nq = s // tm
    d = N_HEADS * HEAD_DIM
    qi_w = IDX_HEADS * IDX_DIM
    kv_w = N_KV_HEADS * HEAD_DIM
    off_qi = d // qi_w
    off_k = (d + qi_w) // kv_w
    off_kw = (d + qi_w + kv_w) // LANES
    row = lambda bb, i: bb * nq + i
    return pl.pallas_call(
        functools.partial(_rope_body, tm=tm),
        grid=(b, nq),
        in_specs=[pl.BlockSpec((tm, d), lambda bb, i: (row(bb, i), 0)),
                  pl.BlockSpec((tm, qi_w), lambda bb, i: (row(bb, i), off_qi)),
                  pl.BlockSpec((tm, kv_w), lambda bb, i: (row(bb, i), off_k)),
                  pl.BlockSpec((tm, LANES), lambda bb, i: (row(bb, i), off_kw)),
                  pl.BlockSpec((tm, 3 * LANES), lambda bb, i: (row(bb, i), 0)),
                  pl.BlockSpec((tm, 3 * LANES), lambda bb, i: (row(bb, i), 0)),
                  pl.BlockSpec((tm, 3 * LANES), lambda bb, i: (row(bb, i), 0))],
        out_specs=[pl.BlockSpec((1, 1, N_HEADS * tm, HEAD_DIM), lambda bb, i: (bb, i, 0, 0)),
                   pl.BlockSpec((1, tm // Q_BLOCK, IDX_HEADS * Q_BLOCK, IDX_DIM), lambda bb, i: (bb, i, 0, 0)),
                   pl.BlockSpec((tm, kv_w), lambda bb, i: (row(bb, i), 0)),
                   pl.BlockSpec((tm, IDX_DIM), lambda bb, i: (row(bb, i), 0)),
                   pl.BlockSpec((tm, IDX_HEADS), lambda bb, i: (row(bb, i), 0))],
        out_shape=[jax.ShapeDtypeStruct((b, nq, N_HEADS * tm, HEAD_DIM), BF16),
                   jax.ShapeDtypeStruct((b, s // Q_BLOCK, IDX_HEADS * Q_BLOCK, IDX_DIM), BF16),
                   jax.ShapeDtypeStruct((t, kv_w), BF16),
                   jax.ShapeDtypeStruct((t, IDX_DIM), BF16),
                   jax.ShapeDtypeStruct((t, IDX_HEADS), F32)],
        compiler_params=_params(("parallel", "parallel"), 48),
        name="rope_split",
    )(zb, zb, zb, zb, tq, ti, tk)


def _index_body(qi_ref, ki_ref, wi_ref, o_ref, skey_ref, *, topk, ch):
    i = pl.program_id(1)
    nch = (i * Q_BLOCK + Q_BLOCK + ch - 1) // ch
    qi = qi_ref[0, 0]
    wi = wi_ref[...]
    row = i * Q_BLOCK + lax.broadcasted_iota(I32, (Q_BLOCK, ch), 0)
    lane = lax.broadcasted_iota(I32, (Q_BLOCK, ch), 1)

    def score_chunk(c, carry):
        kc = ki_ref[pl.ds(pl.multiple_of(c * ch, ch), ch), :]
        logits = lax.dot_general(qi, kc, (((1,), (1,)), ((), ())), preferred_element_type=F32)
        acc = jnp.zeros((Q_BLOCK, ch), F32)
        for h in range(IDX_HEADS):
            acc = acc + jnp.maximum(logits[h * Q_BLOCK:(h + 1) * Q_BLOCK], 0.0) * wi[:, h:h + 1]
        bits = lax.bitcast_convert_type(acc, I32)
        key = bits ^ ((bits >> 31) & 0x7FFFFFFF)
        skey_ref[c] = jnp.where(c * ch + lane <= row, key, INT_MIN)
        return carry

    lax.fori_loop(0, nch, score_chunk, 0)

    def count_ge(cand):
        def body(c, cnt):
            m = jnp.where(skey_ref[c] >= cand, 1.0, 0.0)
            part = m[:, 0:LANES]
            for j in range(1, ch // LANES):
                part = part + m[:, j * LANES:(j + 1) * LANES]
            return cnt + part
        cnt = lax.fori_loop(0, nch, body, jnp.zeros((Q_BLOCK, LANES), F32))
        return jnp.sum(cnt, axis=1, keepdims=True)

    def bit_body(t, u):
        cand_u = u | jnp.left_shift(jnp.int32(1), 31 - t)
        cnt = count_ge(cand_u ^ INT_MIN)
        return jnp.where(cnt >= topk, cand_u, u)

    u = lax.fori_loop(0, 32, bit_body, jnp.zeros((Q_BLOCK, 1), I32))
    thr = jnp.maximum(u ^ INT_MIN, INT_MIN + 1)

    o_ref[...] = jnp.full(o_ref.shape, NEG, F32)

    def write_chunk(c, carry):
        o_ref[0, 0, c] = jnp.where(skey_ref[c] >= thr, 0.0, NEG)
        return carry

    lax.fori_loop(0, nch, write_chunk, 0)


def index_select(qi, ki, wi, b, s, topk, ch):
    nqb = s // Q_BLOCK
    nkc = s // ch
    return pl.pallas_call(
        functools.partial(_index_body, topk=topk, ch=ch),
        grid=(b, nqb),
        in_specs=[pl.BlockSpec((1, 1, IDX_HEADS * Q_BLOCK, IDX_DIM), lambda bb, i: (bb, i, 0, 0)),
                  pl.BlockSpec((s, IDX_DIM), lambda bb, i: (bb, 0)),
                  pl.BlockSpec((Q_BLOCK, IDX_HEADS), lambda bb, i: (bb * nqb + i, 0))],
        out_specs=pl.BlockSpec((1, 1, nkc, Q_BLOCK, ch), lambda bb, i: (bb, i, 0, 0, 0)),
        out_shape=jax.ShapeDtypeStruct((b, nqb, nkc, Q_BLOCK, ch), F32),
        scratch_shapes=[pltpu.VMEM((nkc, Q_BLOCK, ch), I32)],
        compiler_params=_params(("parallel", "arbitrary"), 48),
        name="index_select",
    )(qi, ki, wi)


def _attn_body(q_ref, k_ref, v_ref, b_ref, o_ref, m_scr, l_scr, acc_scr, *, tq, tk):
    i = pl.program_id(1)
    kc = pl.program_id(3)
    last = ((i + 1) * tq - 1) // tk

    @pl.when(kc == 0)
    def _():
        m_scr[...] = jnp.full(m_scr.shape, NEG, F32)
        l_scr[...] = jnp.zeros(l_scr.shape, F32)
        acc_scr[...] = jnp.zeros(acc_scr.shape, F32)

    @pl.when(kc <= last)
    def _():
        q = q_ref[0, 0]
        s = lax.dot_general(q, k_ref[...], (((1,), (1,)), ((), ())), preferred_element_type=F32)
        bias = b_ref[0, :, 0].reshape(tq, tk)
        s = (s.reshape(GROUP, tq, tk) + bias[None]).reshape(GROUP * tq, tk)
        m_prev = m_scr[...]
        m_new = jnp.maximum(m_prev, jnp.max(s, axis=1, keepdims=True))
        alpha = jnp.exp(m_prev - m_new)
        p = jnp.exp(s - m_new)
        l_scr[...] = alpha * l_scr[...] + jnp.sum(p, axis=1, keepdims=True)
        acc_scr[...] = alpha * acc_scr[...] + jnp.dot(p.astype(BF16), v_ref[...], preferred_element_type=F32)
        m_scr[...] = m_new

    @pl.when(kc == last)
    def _():
        o = acc_scr[...] / l_scr[...]
        for h in range(GROUP):
            o_ref[:, h * HEAD_DIM:(h + 1) * HEAD_DIM] = o[h * tq:(h + 1) * tq].astype(o_ref.dtype)


def masked_attention(q, k, za, v_col0, bias, b, s, tq, tk):
    nqt = s // tq
    nkc = s // tk
    t = b * s

    def kidx(i, kc):
        return jnp.minimum(kc, ((i + 1) * tq - 1) // tk)

    return pl.pallas_call(
        functools.partial(_attn_body, tq=tq, tk=tk),
        grid=(b, nqt, N_KV_HEADS, nkc),
        in_specs=[pl.BlockSpec((1, 1, GROUP * tq, HEAD_DIM), lambda bb, i, g, kc: (bb, i, g, 0)),
                  pl.BlockSpec((tk, HEAD_DIM), lambda bb, i, g, kc: (bb * nkc + kidx(i, kc), g)),
                  pl.BlockSpec((tk, HEAD_DIM), lambda bb, i, g, kc: (bb * nkc + kidx(i, kc), v_col0 + g)),
                  pl.BlockSpec((1, tq // Q_BLOCK, 1, Q_BLOCK, tk),
                               lambda bb, i, g, kc: (bb, i, kidx(i, kc), 0, 0))],
        out_specs=pl.BlockSpec((tq, GROUP * HEAD_DIM), lambda bb, i, g, kc: (bb * nqt + i, g)),
        out_shape=jax.ShapeDtypeStruct((t, N_HEADS * HEAD_DIM), BF16),
        scratch_shapes=[pltpu.VMEM((GROUP * tq, 1), F32),
                        pltpu.VMEM((GROUP * tq, 1), F32),
                        pltpu.VMEM((GROUP * tq, HEAD_DIM), F32)],
        compiler_params=_params(("parallel", "parallel", "parallel", "arbitrary"), 48),
        name="masked_attention",
    )(q, k, za, bias)


def _merge_body(cb_ref, cc_ref, cx_ref, gc_ref, ga_ref, cch_ref, cxh_ref, ya_ref, cw_ref, wo_ref, x_ref,
                o_ref, *, tm, tiles_per_seq):
    i = pl.program_id(0)
    p = cc_ref[...].astype(F32) * cx_ref[...].astype(F32)
    halo = cch_ref[...].astype(F32) * cxh_ref[...].astype(F32)
    halo = jnp.where(i % tiles_per_seq == 0, 0.0, halo)
    row = lax.broadcasted_iota(I32, p.shape, 0)
    p1 = jnp.where(row == 0, halo[7:8], pltpu.roll(p, 1, 0))
    p2 = jnp.where(row == 0, halo[6:7], jnp.where(row == 1, halo[7:8], pltpu.roll(p, 2, 0)))
    cw = cw_ref[...]
    conv = cw[0:1] * p2 + cw[1:2] * p1 + cw[2:3] * p
    merged = (jax.nn.sigmoid(gc_ref[...].astype(F32)) * (cb_ref[...].astype(F32) * conv)
              + jax.nn.sigmoid(ga_ref[...].astype(F32)) * ya_ref[...].astype(F32))
    o_ref[...] = x_ref[...] + jnp.dot(merged.astype(BF16), wo_ref[...], preferred_element_type=F32)


def merge_outproj(za, y_attn, conv_w, w_out, x, s, tm):
    t, c = x.shape
    hb = tm // 8
    col = lambda j: (lambda i: (i, j))
    halo = lambda j: (lambda i: (jnp.maximum(i * hb - 1, 0), j))
    return pl.pallas_call(
        functools.partial(_merge_body, tm=tm, tiles_per_seq=s // tm),
        grid=(t // tm,),
        in_specs=[pl.BlockSpec((tm, c), col(0)), pl.BlockSpec((tm, c), col(1)), pl.BlockSpec((tm, c), col(2)),
                  pl.BlockSpec((tm, c), col(3)), pl.BlockSpec((tm, c), col(4)),
                  pl.BlockSpec((8, c), halo(1)), pl.BlockSpec((8, c), halo(2)),
                  pl.BlockSpec((tm, c), lambda i: (i, 0)),
                  pl.BlockSpec((CONV_WIDTH, c), lambda i: (0, 0)),
                  pl.BlockSpec((c, c), lambda i: (0, 0)),
                  pl.BlockSpec((tm, c), lambda i: (i, 0))],
        out_specs=pl.BlockSpec((tm, c), lambda i: (i, 0)),
        out_shape=jax.ShapeDtypeStruct((t, c), F32),
        compiler_params=_params(("parallel",), 56),
        name="merge_outproj",
    )(za, za, za, za, za, za, za, y_attn, conv_w, w_out, x)


def _xattn_body(h_ref, g_ref, wq_ref, kv_ref, wo_ref, o_ref):
    h = h_ref[...]
    hn = _rms(h, g_ref[...]).astype(BF16)
    q = jnp.dot(hn, wq_ref[...], preferred_element_type=F32) * (X_HEAD_DIM ** -0.5)
    q = q.astype(BF16)
    kv = kv_ref[...]
    kw = X_HEADS * X_HEAD_DIM
    outs = []
    for hh in range(X_HEADS):
        sl = slice(hh * X_HEAD_DIM, (hh + 1) * X_HEAD_DIM)
        s = lax.dot_general(q[:, sl], kv[:, sl], (((1,), (1,)), ((), ())), preferred_element_type=F32)
        m = jnp.max(s, axis=1, keepdims=True)
        p = jnp.exp(s - m)
        l = jnp.sum(p, axis=1, keepdims=True)
        vh = kv[:, kw + hh * X_HEAD_DIM: kw + (hh + 1) * X_HEAD_DIM]
        outs.append((jnp.dot(p.astype(BF16), vh, preferred_element_type=F32) / l).astype(BF16))
    o = jnp.concatenate(outs, axis=1)
    o_ref[...] = h + jnp.dot(o, wo_ref[...], preferred_element_type=F32)


def cross_attention(h, g, wq, kv, wo, s, tm):
    t, d = h.shape
    m = kv.shape[0] // (t // s)
    kw = X_HEADS * X_HEAD_DIM
    tps = s // tm
    return pl.pallas_call(
        _xattn_body,
        grid=(t // tm,),
        in_specs=[pl.BlockSpec((tm, d), lambda i: (i, 0)),
                  pl.BlockSpec((1, d), lambda i: (0, 0)),
                  pl.BlockSpec((d, kw), lambda i: (0, 0)),
                  pl.BlockSpec((m, 2 * kw), lambda i: (i // tps, 0)),
                  pl.BlockSpec((kw, d), lambda i: (0, 0))],
        out_specs=pl.BlockSpec((tm, d), lambda i: (i, 0)),
        out_shape=jax.ShapeDtypeStruct((t, d), F32),
        compiler_params=_params(("parallel",), 48),
        name="cross_attention",
    )(h, g.reshape(1, d), wq, kv, wo)


def _router_body(h_ref, g_ref, wr_ref, br_ref, hf_ref, idx_ref, gate_ref):
    hf = _rms(h_ref[...], g_ref[...])
    hf_ref[...] = hf.astype(BF16)
    logits = jnp.dot(hf, wr_ref[...], preferred_element_type=F32, precision=lax.Precision.HIGHEST) + br_ref[...]
    lane = lax.broadcasted_iota(I32, logits.shape, 1)
    lane_f = lane.astype(F32)
    l = jnp.where(lane < N_EXPERTS, logits, -jnp.inf)
    vals, idxs = [], []
    for _ in range(TOP_K_EXPERTS):
        m = jnp.max(l, axis=1, keepdims=True)
        ix = jnp.min(jnp.where(l == m, lane_f, float(LANES)), axis=1, keepdims=True)
        vals.append(m)
        idxs.append(ix)
        l = jnp.where(lane_f == ix, -jnp.inf, l)
    es = [jnp.exp(v - vals[0]) for v in vals]
    den = es[0] + es[1] + es[2] + es[3]
    idx_o = jnp.zeros(logits.shape, F32)
    gate_o = jnp.zeros(logits.shape, F32)
    for kk in range(TOP_K_EXPERTS):
        idx_o = jnp.where(lane == kk, idxs[kk], idx_o)
        gate_o = jnp.where(lane == kk, es[kk] / den, gate_o)
    idx_ref[...] = idx_o.astype(I32)
    gate_ref[...] = gate_o


def ffn_router(h, g, w_router, b_router, tm):
    t, d = h.shape
    wr = jnp.zeros((d, LANES), F32).at[:, :N_EXPERTS].set(w_router)
    br = jnp.zeros((1, LANES), F32).at[0, :N_EXPERTS].set(b_router)
    return pl.pallas_call(
        _router_body,
        grid=(t // tm,),
        in_specs=[pl.BlockSpec((tm, d), lambda i: (i, 0)),
                  pl.BlockSpec((1, d), lambda i: (0, 0)),
                  pl.BlockSpec((d, LANES), lambda i: (0, 0)),
                  pl.BlockSpec((1, LANES), lambda i: (0, 0))],
        out_specs=[pl.BlockSpec((tm, d), lambda i: (i, 0)),
                   pl.BlockSpec((tm, LANES), lambda i: (i, 0)),
                   pl.BlockSpec((tm, LANES), lambda i: (i, 0))],
        out_shape=[jax.ShapeDtypeStruct((t, d), BF16),
                   jax.ShapeDtypeStruct((t, LANES), I32),
                   jax.ShapeDtypeStruct((t, LANES), F32)],
        compiler_params=_params(("parallel",), 48),
        name="ffn_router",
    )(h, g.reshape(1, d), wr, br)


def _expert_changed(te_ref, i):
    return (i == 0) | (te_ref[i] != te_ref[jnp.maximum(i - 1, 0)])


def _gmm1_body(te_ref, nu_ref, x_ref, wg_ref, wu_ref, bg_ref, bu_ref, o_ref, wg_s, wu_s):
    i = pl.program_id(1)

    @pl.when(i < nu_ref[0])
    def _():
        @pl.when(_expert_changed(te_ref, i))
        def _():
            wg_s[...] = wg_ref[0].astype(BF16)
            wu_s[...] = wu_ref[0].astype(BF16)

        x = x_ref[...]
        gate = jnp.dot(x, wg_s[...], preferred_element_type=F32) + bg_ref[0]
        up = jnp.dot(x, wu_s[...], preferred_element_type=F32) + bu_ref[0]
        gate = jnp.minimum(gate, SWIGLU_LIMIT)
        up = jnp.clip(up, -SWIGLU_LIMIT, SWIGLU_LIMIT)
        o_ref[...] = ((up + 1.0) * (gate * jax.nn.sigmoid(SWIGLU_ALPHA * gate))).astype(o_ref.dtype)


def _gmm2_body(te_ref, nu_ref, h_ref, wd_ref, bd_ref, o_ref, wd_s):
    i = pl.program_id(1)

    @pl.when(i < nu_ref[0])
    def _():
        @pl.when(_expert_changed(te_ref, i))
        def _():
            wd_s[...] = wd_ref[0].astype(BF16)

        o_ref[...] = (jnp.dot(h_ref[...], wd_s[...], preferred_element_type=F32) + bd_ref[0]).astype(o_ref.dtype)


def expert_ffn(xs, tile_e, n_used, w_gu, b_gu, w_dn, b_dn, tf, tn):
    cap, d = xs.shape
    dff = w_dn.shape[1]
    n_tiles = cap // EXPERT_BLOCK
    nj = dff // tf
    tile = lambda i, nu: jnp.minimum(i, nu[0] - 1)
    b_gu3 = b_gu.reshape(N_EXPERTS, 1, 2 * dff)
    b_dn3 = b_dn.reshape(N_EXPERTS, 1, d)
    hdn = pl.pallas_call(
        _gmm1_body,
        grid_spec=pltpu.PrefetchScalarGridSpec(
            num_scalar_prefetch=2,
            grid=(nj, n_tiles),
            in_specs=[pl.BlockSpec((EXPERT_BLOCK, d), lambda j, i, te, nu: (tile(i, nu), 0)),
                      pl.BlockSpec((1, d, tf), lambda j, i, te, nu: (te[tile(i, nu)], 0, j)),
                      pl.BlockSpec((1, d, tf), lambda j, i, te, nu: (te[tile(i, nu)], 0, nj + j)),
                      pl.BlockSpec((1, 1, tf), lambda j, i, te, nu: (te[tile(i, nu)], 0, j)),
                      pl.BlockSpec((1, 1, tf), lambda j, i, te, nu: (te[tile(i, nu)], 0, nj + j))],
            out_specs=pl.BlockSpec((EXPERT_BLOCK, tf), lambda j, i, te, nu: (tile(i, nu), j)),
            scratch_shapes=[pltpu.VMEM((d, tf), BF16), pltpu.VMEM((d, tf), BF16)]),
        out_shape=jax.ShapeDtypeStruct((cap, dff), BF16),
        compiler_params=_params(("arbitrary", "arbitrary"), 56),
        name="expert_gate_up",
    )(tile_e, n_used, xs, w_gu, w_gu, b_gu3, b_gu3)
    nn = d // tn
    return pl.pallas_call(
        _gmm2_body,
        grid_spec=pltpu.PrefetchScalarGridSpec(
            num_scalar_prefetch=2,
            grid=(nn, n_tiles),
            in_specs=[pl.BlockSpec((EXPERT_BLOCK, dff), lambda j, i, te, nu: (tile(i, nu), 0)),
                      pl.BlockSpec((1, dff, tn), lambda j, i, te, nu: (te[tile(i, nu)], 0, j)),
                      pl.BlockSpec((1, 1, tn), lambda j, i, te, nu: (te[tile(i, nu)], 0, j))],
            out_specs=pl.BlockSpec((EXPERT_BLOCK, tn), lambda j, i, te, nu: (tile(i, nu), j)),
            scratch_shapes=[pltpu.VMEM((dff, tn), BF16)]),
        out_shape=jax.ShapeDtypeStruct((cap, d), BF16),
        compiler_params=_params(("arbitrary", "arbitrary"), 56),
        name="expert_down",
    )(tile_e, n_used, hdn, w_dn, b_dn3)


def _combine_body(h_ref, y_ref, gate_ref, g_ref, o_ref, *, final):
    d = h_ref.shape[1]
    acc = h_ref[...]
    gate = gate_ref[...]
    for kk in range(TOP_K_EXPERTS):
        acc = acc + gate[:, kk:kk + 1] * y_ref[:, kk * d:(kk + 1) * d].astype(F32)
    o_ref[...] = _rms(acc, g_ref[...]) if final else acc


def combine_norm(h, yg, gates, g, tm, final):
    t, d = h.shape
    return pl.pallas_call(
        functools.partial(_combine_body, final=final),
        grid=(t // tm,),
        in_specs=[pl.BlockSpec((tm, d), lambda i: (i, 0)),
                  pl.BlockSpec((tm, TOP_K_EXPERTS * d), lambda i: (i, 0)),
                  pl.BlockSpec((tm, LANES), lambda i: (i, 0)),
                  pl.BlockSpec((1, d), lambda i: (0, 0))],
        out_specs=pl.BlockSpec((tm, d), lambda i: (i, 0)),
        out_shape=jax.ShapeDtypeStruct((t, d), F32),
        compiler_params=_params(("parallel",), 48),
        name="combine_norm",
    )(h, yg, gates, g.reshape(1, d))


def _rope_table(positions, rot_dim, head_dim, active_lanes):
    half = rot_dim // 2
    inv_freq = jnp.float32(ROPE_THETA) ** (-(jnp.arange(half, dtype=F32) * 2.0 / rot_dim))
    ang = positions.astype(F32).reshape(-1)[:, None] * inv_freq
    cos, sin = jnp.cos(ang), jnp.sin(ang)
    t = cos.shape[0]
    one = jnp.ones((t, head_dim - rot_dim), F32)
    zero = jnp.zeros((t, head_dim - rot_dim), F32)
    zh = jnp.zeros((t, half), F32)
    c = jnp.concatenate([cos, cos, one], axis=1)
    s1 = jnp.concatenate([zh, sin, zero], axis=1)
    s2 = jnp.concatenate([-sin, zh, zero], axis=1)
    reps = active_lanes // head_dim

    def widen(a, fill):
        a = jnp.tile(a, (1, reps))
        return jnp.concatenate([a, jnp.full((t, LANES - active_lanes), fill, F32)], axis=1)

    return jnp.concatenate([widen(c, 1.0), widen(s1, 0.0), widen(s2, 0.0)], axis=1)


def _route(idx, t, n_tiles):
    flat_e = idx.reshape(-1)
    onehot = (flat_e[:, None] == jnp.arange(N_EXPERTS, dtype=I32)[None, :]).astype(I32)
    csum = jnp.cumsum(onehot, axis=0)
    rank = jnp.take_along_axis(csum, flat_e[:, None], axis=1)[:, 0] - 1
    counts = csum[-1]
    padded = (counts + EXPERT_BLOCK - 1) // EXPERT_BLOCK * EXPERT_BLOCK
    pend = jnp.cumsum(padded)
    pstart = pend - padded
    dest = pstart[flat_e] + rank
    cap = n_tiles * EXPERT_BLOCK
    buf_tok = jnp.full((cap,), t, I32).at[dest].set(jnp.arange(flat_e.shape[0], dtype=I32) // TOP_K_EXPERTS)
    tile_e = jnp.minimum(jnp.searchsorted(pend, jnp.arange(n_tiles, dtype=I32) * EXPERT_BLOCK, side='right'),
                         N_EXPERTS - 1).astype(I32)
    n_used = (pend[-1] // EXPERT_BLOCK).astype(I32).reshape(1)
    return dest, buf_tok, tile_e, n_used


def kernel(x, mem, positions, norm_mix, w_in, conv_w, w_out, norm_xattn, norm_mem, wq_x, wk_x, wv_x, wo_x,
           norm_ffn, w_router, b_router, w_gate_up, b_gate_up, w_down, b_down, norm_final):
    b, s, d = x.shape
    t = b * s
    assert d == N_HEADS * HEAD_DIM and s % 512 == 0
    x2 = x.reshape(t, d)
    h = x2
    for l in range(w_in.shape[0]):
        wl = w_in[l]
        o = [0]
        for n in (d, d, d, d, N_KV_HEADS * HEAD_DIM, N_KV_HEADS * HEAD_DIM, IDX_HEADS * IDX_DIM, IDX_DIM, IDX_HEADS, d, d):
            o.append(o[-1] + n)
        seg = lambda a: wl[:, o[a]:o[a + 1]]
        w_a = jnp.concatenate([seg(0), seg(1), seg(2), seg(9), seg(10), seg(5)], axis=1).astype(BF16)
        pad_b = jnp.zeros((d, LANES - IDX_DIM - IDX_HEADS + 128), F32)
        w_b = jnp.concatenate([seg(3), seg(6), seg(4), seg(7), seg(8), pad_b], axis=1).astype(BF16)
        za = norm_matmul(h, norm_mix[l], w_a, BF16, 1024, 512)
        zb = norm_matmul(h, norm_mix[l], w_b, F32, 1024, 768)
        tq = _rope_table(positions, ROT_DIM, HEAD_DIM, LANES)
        ti = _rope_table(positions, IDX_ROT_DIM, IDX_DIM, LANES)
        tk = _rope_table(positions, IDX_ROT_DIM, IDX_DIM, IDX_DIM)
        tq_rows = 256
        q_r, qi_r, k_r, ki_r, wi_r = rope_split(zb, tq, ti, tk, b, s, tq_rows)
        topk = min(TOPK_MAX, s // 4)
        tkc = 512
        bias = index_select(qi_r, ki_r, wi_r, b, s, topk, tkc)
        y_attn = masked_attention(q_r, k_r, za, 5 * d // HEAD_DIM, bias, b, s, tq_rows, tkc)
        h = merge_outproj(za, y_attn, conv_w[l], w_out[l].astype(BF16), h, s, 256)
        kv_w = jnp.concatenate([wk_x[l], wv_x[l]], axis=1).astype(BF16)
        kv = norm_matmul(mem.reshape(-1, d), norm_mem[l], kv_w, BF16, 1024, 512)
        h = cross_attention(h, norm_xattn[l], wq_x[l].astype(BF16), kv, wo_x[l].astype(BF16), s, 256)
        hf, idx, gates = ffn_router(h, norm_ffn[l], w_router[l], b_router[l], 256)
        n_tiles = -(-(t * TOP_K_EXPERTS) // EXPERT_BLOCK) + N_EXPERTS
        dest, buf_tok, tile_e, n_used = _route(idx[:, :TOP_K_EXPERTS], t, n_tiles)
        hf_pad = jnp.concatenate([hf, jnp.zeros((1, d), BF16)], axis=0)
        xs = hf_pad[buf_tok]
        y = expert_ffn(xs, tile_e, n_used, w_gate_up[l], b_gate_up[l], w_down[l], b_down[l], 512, 1024)
        yg = y[dest].reshape(t, TOP_K_EXPERTS * d)
        h = combine_norm(h, yg, gates, norm_final, 256, final=(l + 1 == w_in.shape[0]))
    return h.reshape(b, s, d)
```

```python
import functools

import jax
import jax.numpy as jnp
from jax import lax
from jax.experimental import pallas as pl
from jax.experimental.pallas import tpu as pltpu

F32 = jnp.float32
BF16 = jnp.bfloat16
I32 = jnp.int32

HEAD_DIM = 128
N_HEADS = 16
N_KV_HEADS = 4
GROUP = N_HEADS // N_KV_HEADS
ROT_DIM = 32
IDX_HEADS = 16
IDX_DIM = 64
IDX_ROT_DIM = 16
TOPK_MAX = 256
Q_BLOCK = 128
ROPE_THETA = 500000.0
X_HEADS = 4
X_HEAD_DIM = 128
N_EXPERTS = 32
TOP_K_EXPERTS = 4
SWIGLU_LIMIT = 7.0
SWIGLU_ALPHA = 1.702
EXPERT_BLOCK = 256
NORM_EPS = 1e-6
CONV_WIDTH = 3

LANES = 128
NEG = -1e30
INT_MIN = -2 ** 31
MIB = 1024 * 1024
LOG2E = 1.4426950408889634


def _params(sem, vmem_mib):
    return pltpu.CompilerParams(dimension_semantics=sem, vmem_limit_bytes=vmem_mib * MIB)


def _rms(x, g):
    ms = jnp.mean(x * x, axis=-1, keepdims=True)
    return x * lax.rsqrt(ms + NORM_EPS) * g


def _norm_matmul_body(x_ref, g_ref, w_ref, o_ref, a_scr):
    @pl.when(pl.program_id(1) == 0)
    def _():
        a_scr[...] = _rms(x_ref[...], g_ref[...]).astype(BF16)

    o_ref[...] = jnp.dot(a_scr[...], w_ref[...], preferred_element_type=F32).astype(o_ref.dtype)


def norm_matmul(x, g, w, out_dtype, tm, tn):
    m, k = x.shape
    n = w.shape[1]
    tm = min(tm, m)
    assert m % tm == 0 and n % tn == 0
    return pl.pallas_call(
        _norm_matmul_body,
        grid=(m // tm, n // tn),
        in_specs=[pl.BlockSpec((tm, k), lambda i, j: (i, 0)),
                  pl.BlockSpec((1, k), lambda i, j: (0, 0)),
                  pl.BlockSpec((k, tn), lambda i, j: (0, j))],
        out_specs=pl.BlockSpec((tm, tn), lambda i, j: (i, j)),
        out_shape=jax.ShapeDtypeStruct((m, n), out_dtype),
        scratch_shapes=[pltpu.VMEM((tm, k), BF16)],
        compiler_params=_params(("parallel", "arbitrary"), 48),
        name="norm_matmul",
    )(x, g.reshape(1, k), w)


def _rope(x, tab, shift):
    c, s1, s2 = tab[:, 0:LANES], tab[:, LANES:2 * LANES], tab[:, 2 * LANES:3 * LANES]
    return x * c + pltpu.roll(x, shift, 1) * s1 + pltpu.roll(x, LANES - shift, 1) * s2


def _rope_body(q_ref, qi_ref, k_ref, kw_ref, tq_ref, ti_ref, tk_ref,
               qo_ref, qio_ref, ko_ref, kio_ref, wio_ref, *, tm):
    tq = tq_ref[...]
    scale = HEAD_DIM ** -0.5 * LOG2E
    for h in range(N_HEADS):
        xh = q_ref[:, h * LANES:(h + 1) * LANES]
        qo_ref[0, 0, h * tm:(h + 1) * tm, :] = (_rope(xh, tq, ROT_DIM // 2) * scale).astype(BF16)
    for g in range(N_KV_HEADS):
        xh = k_ref[:, g * LANES:(g + 1) * LANES]
        ko_ref[:, g * LANES:(g + 1) * LANES] = _rope(xh, tq, ROT_DIM // 2).astype(BF16)
    ti = ti_ref[...]
    for j in range(IDX_HEADS * IDX_DIM // LANES):
        xr = _rope(qi_ref[:, j * LANES:(j + 1) * LANES], ti, IDX_ROT_DIM // 2).astype(BF16)
        for half in range(LANES // IDX_DIM):
            h = j * (LANES // IDX_DIM) + half
            part = xr[:, half * IDX_DIM:(half + 1) * IDX_DIM]
            for qb in range(tm // Q_BLOCK):
                qio_ref[0, qb, h * Q_BLOCK:(h + 1) * Q_BLOCK, :] = part[qb * Q_BLOCK:(qb + 1) * Q_BLOCK]
    kw = kw_ref[...]
    kr = _rope(kw, tk_ref[...], IDX_ROT_DIM // 2)
    kio_ref[...] = kr[:, 0:IDX_DIM].astype(BF16)
    wio_ref[...] = kw[:, IDX_DIM:IDX_DIM + IDX_HEADS] * (IDX_HEADS ** -0.5 * IDX_DIM ** -0.5)


def rope_split(zb, tq, ti, tk, b, s, tm):
    t = b * s
    nq = s // tm
    d = N_HEADS * HEAD_DIM
    qi_w = IDX_HEADS * IDX_DIM
    kv_w = N_KV_HEADS * HEAD_DIM
    off_qi = d // qi_w
    off_k = (d + qi_w) // kv_w
    off_kw = (d + qi_w + kv_w) // LANES
    row = lambda bb, i: bb * nq + i
    return pl.pallas_call(
        functools.partial(_rope_body, tm=tm),
        grid=(b, nq),
        in_specs=[pl.BlockSpec((tm, d), lambda bb, i: (row(bb, i), 0)),
                  pl.BlockSpec((tm, qi_w), lambda bb, i: (row(bb, i), off_qi)),
                  pl.BlockSpec((tm, kv_w), lambda bb, i: (row(bb, i), off_k)),
                  pl.BlockSpec((tm, LANES), lambda bb, i: (row(bb, i), off_kw)),
                  pl.BlockSpec((tm, 3 * LANES), lambda bb, i: (row(bb, i), 0)),
                  pl.BlockSpec((tm, 3 * LANES), lambda bb, i: (row(bb, i), 0)),
                  pl.BlockSpec((tm, 3 * LANES), lambda bb, i: (row(bb, i), 0))],
        out_specs=[pl.BlockSpec((1, 1, N_HEADS * tm, HEAD_DIM), lambda bb, i: (bb, i, 0, 0)),
                   pl.BlockSpec((1, tm // Q_BLOCK, IDX_HEADS * Q_BLOCK, IDX_DIM), lambda bb, i: (bb, i, 0, 0)),
                   pl.BlockSpec((tm, kv_w), lambda bb, i: (row(bb, i), 0)),
                   pl.BlockSpec((tm, IDX_DIM), lambda bb, i: (row(bb, i), 0)),
                   pl.BlockSpec((tm, IDX_HEADS), lambda bb, i: (row(bb, i), 0))],
        out_shape=[jax.ShapeDtypeStruct((b, nq, N_HEADS * tm, HEAD_DIM), BF16),
                   jax.ShapeDtypeStruct((b, s // Q_BLOCK, IDX_HEADS * Q_BLOCK, IDX_DIM), BF16),
                   jax.ShapeDtypeStruct((t, kv_w), BF16),
                   jax.ShapeDtypeStruct((t, IDX_DIM), BF16),
                   jax.ShapeDtypeStruct((t, IDX_HEADS), F32)],
        compiler_params=_params(("parallel", "parallel"), 48),
        name="rope_split",
    )(zb, zb, zb, zb, tq, ti, tk)


def _index_body(qi_ref, ki_ref, wi_ref, o_ref, skey_ref, *, topk, ch):
    i = pl.program_id(1)
    nch = (i * Q_BLOCK + Q_BLOCK + ch - 1) // ch
    qi = qi_ref[0, 0]
    wi = wi_ref[0, 0]
    key_pos = lax.broadcasted_iota(I32, (ch, Q_BLOCK), 0)
    q_pos = i * Q_BLOCK + lax.broadcasted_iota(I32, (ch, Q_BLOCK), 1)

    def score_chunk(c, carry):
        kc = ki_ref[pl.ds(pl.multiple_of(c * ch, ch), ch), :]
        logits = lax.dot_general(kc, qi, (((1,), (1,)), ((), ())), preferred_element_type=F32)
        acc = jnp.zeros((ch, Q_BLOCK), F32)
        for h in range(IDX_HEADS):
            acc = acc + jnp.maximum(logits[:, h * Q_BLOCK:(h + 1) * Q_BLOCK], 0.0) * wi[h:h + 1, :]
        bits = lax.bitcast_convert_type(acc, I32)
        key = bits ^ ((bits >> 31) & 0x7FFFFFFF)
        skey_ref[c] = jnp.where(c * ch + key_pos <= q_pos, key, INT_MIN)
        return carry

    lax.fori_loop(0, nch, score_chunk, 0)

    def count_ge(cand):
        def body(c, cnt):
            m = jnp.where(skey_ref[c] >= cand, 1.0, 0.0)
            return cnt + jnp.sum(m.reshape(ch // 64, 64, Q_BLOCK), axis=0)
        cnt = lax.fori_loop(0, nch, body, jnp.zeros((64, Q_BLOCK), F32))
        return jnp.sum(cnt, axis=0, keepdims=True)

    def bit_body(t, u):
        cand_u = u | jnp.left_shift(jnp.int32(1), 31 - t)
        cnt = count_ge(cand_u ^ INT_MIN)
        return jnp.where(cnt >= topk, cand_u, u)

    u = lax.fori_loop(0, 32, bit_body, jnp.zeros((1, Q_BLOCK), I32))
    thr = jnp.maximum(u ^ INT_MIN, INT_MIN + 1)

    o_ref[...] = jnp.full(o_ref.shape, NEG, F32)

    def write_chunk(c, carry):
        o_ref[0, 0, c] = jnp.where(skey_ref[c] >= thr, 0.0, NEG).T
        return carry

    lax.fori_loop(0, nch, write_chunk, 0)


def index_select(qi, ki, wi_t, b, s, topk, ch):
    nqb = s // Q_BLOCK
    nkc = s // ch
    return pl.pallas_call(
        functools.partial(_index_body, topk=topk, ch=ch),
        grid=(b, nqb),
        in_specs=[pl.BlockSpec((1, 1, IDX_HEADS * Q_BLOCK, IDX_DIM), lambda bb, i: (bb, i, 0, 0)),
                  pl.BlockSpec((s, IDX_DIM), lambda bb, i: (bb, 0)),
                  pl.BlockSpec((1, 1, IDX_HEADS, Q_BLOCK), lambda bb, i: (bb, i, 0, 0))],
        out_specs=pl.BlockSpec((1, 1, nkc, Q_BLOCK, ch), lambda bb, i: (bb, i, 0, 0, 0)),
        out_shape=jax.ShapeDtypeStruct((b, nqb, nkc, Q_BLOCK, ch), F32),
        scratch_shapes=[pltpu.VMEM((nkc, ch, Q_BLOCK), I32)],
        compiler_params=_params(("parallel", "arbitrary"), 48),
        name="index_select",
    )(qi, ki, wi_t)


def _attn_body(pi_ref, pk_ref, q_ref, k_ref, v_ref, b_ref, o_ref, m_scr, l_scr, acc_scr, *, tq, tk):
    step = pl.program_id(2)
    i = pi_ref[step]
    kc = pk_ref[step]
    last = ((i + 1) * tq - 1) // tk

    @pl.when(kc == 0)
    def _():
        m_scr[...] = jnp.full(m_scr.shape, NEG, F32)
        l_scr[...] = jnp.zeros(l_scr.shape, F32)
        acc_scr[...] = jnp.zeros(acc_scr.shape, F32)

    k = k_ref[...]
    v = v_ref[...]
    bias = b_ref[0, :, 0].reshape(tq, tk)
    for h in range(GROUP):
        q = q_ref[0, 0, h * tq:(h + 1) * tq, :]
        s = lax.dot_general(q, k, (((1,), (1,)), ((), ())), preferred_element_type=F32) + bias
        m_prev = m_scr[h]
        m_new = jnp.maximum(m_prev, jnp.max(s, axis=1, keepdims=True))
        alpha = jnp.exp2(m_prev - m_new)
        p = jnp.exp2(s - jnp.concatenate([m_new] * (tk // LANES), axis=1))
        l_scr[h] = alpha * l_scr[h] + jnp.sum(p, axis=1, keepdims=True)
        acc_scr[h] = alpha * acc_scr[h] + jnp.dot(p.astype(BF16), v, preferred_element_type=F32)
        m_scr[h] = m_new

    @pl.when(kc == last)
    def _():
        for h in range(GROUP):
            o_ref[:, h * HEAD_DIM:(h + 1) * HEAD_DIM] = (acc_scr[h] / l_scr[h]).astype(o_ref.dtype)


def masked_attention(q, k, za, v_col0, bias, b, s, tq, tk):
    nqt = s // tq
    nkc = s // tk
    t = b * s
    pairs = [(i, kc) for i in range(nqt) for kc in range(((i + 1) * tq - 1) // tk + 1)]
    pi = jnp.asarray([p[0] for p in pairs], I32)
    pk = jnp.asarray([p[1] for p in pairs], I32)
    return pl.pallas_call(
        functools.partial(_attn_body, tq=tq, tk=tk),
        grid_spec=pltpu.PrefetchScalarGridSpec(
            num_scalar_prefetch=2,
            grid=(b, N_KV_HEADS, len(pairs)),
            in_specs=[pl.BlockSpec((1, 1, GROUP * tq, HEAD_DIM), lambda bb, g, p, pi, pk: (bb, pi[p], g, 0)),
                      pl.BlockSpec((tk, HEAD_DIM), lambda bb, g, p, pi, pk: (bb * nkc + pk[p], g)),
                      pl.BlockSpec((tk, HEAD_DIM), lambda bb, g, p, pi, pk: (bb * nkc + pk[p], v_col0 + g)),
                      pl.BlockSpec((1, tq // Q_BLOCK, 1, Q_BLOCK, tk),
                                   lambda bb, g, p, pi, pk: (bb, pi[p], pk[p], 0, 0))],
            out_specs=pl.BlockSpec((tq, GROUP * HEAD_DIM), lambda bb, g, p, pi, pk: (bb * nqt + pi[p], g)),
            scratch_shapes=[pltpu.VMEM((GROUP, tq, LANES), F32),
                            pltpu.VMEM((GROUP, tq, LANES), F32),
                            pltpu.VMEM((GROUP, tq, HEAD_DIM), F32)]),
        out_shape=jax.ShapeDtypeStruct((t, N_HEADS * HEAD_DIM), BF16),
        compiler_params=_params(("parallel", "parallel", "arbitrary"), 48),
        name="masked_attention",
    )(pi, pk, q, k, za, bias)


def _merge_body(cb_ref, cc_ref, cx_ref, gc_ref, ga_ref, cch_ref, cxh_ref, ya_ref, cw_ref, wo_ref, x_ref,
                o_ref, *, tm, tiles_per_seq):
    i = pl.program_id(0)
    p = cc_ref[...].astype(F32) * cx_ref[...].astype(F32)
    halo = cch_ref[...].astype(F32) * cxh_ref[...].astype(F32)
    halo = jnp.where(i % tiles_per_seq == 0, 0.0, halo)
    row = lax.broadcasted_iota(I32, p.shape, 0)
    p1 = jnp.where(row == 0, halo[7:8], pltpu.roll(p, 1, 0))
    p2 = jnp.where(row == 0, halo[6:7], jnp.where(row == 1, halo[7:8], pltpu.roll(p, 2, 0)))
    cw = cw_ref[...]
    conv = cw[0:1] * p2 + cw[1:2] * p1 + cw[2:3] * p
    merged = (jax.nn.sigmoid(gc_ref[...].astype(F32)) * (cb_ref[...].astype(F32) * conv)
              + jax.nn.sigmoid(ga_ref[...].astype(F32)) * ya_ref[...].astype(F32))
    o_ref[...] = x_ref[...] + jnp.dot(merged.astype(BF16), wo_ref[...], preferred_element_type=F32)


def merge_outproj(za, y_attn, conv_w, w_out, x, s, tm):
    t, c = x.shape
    hb = tm // 8
    col = lambda j: (lambda i: (i, j))
    halo = lambda j: (lambda i: (jnp.maximum(i * hb - 1, 0), j))
    return pl.pallas_call(
        functools.partial(_merge_body, tm=tm, tiles_per_seq=s // tm),
        grid=(t // tm,),
        in_specs=[pl.BlockSpec((tm, c), col(0)), pl.BlockSpec((tm, c), col(1)), pl.BlockSpec((tm, c), col(2)),
                  pl.BlockSpec((tm, c), col(3)), pl.BlockSpec((tm, c), col(4)),
                  pl.BlockSpec((8, c), halo(1)), pl.BlockSpec((8, c), halo(2)),
                  pl.BlockSpec((tm, c), lambda i: (i, 0)),
                  pl.BlockSpec((CONV_WIDTH, c), lambda i: (0, 0)),
                  pl.BlockSpec((c, c), lambda i: (0, 0)),
                  pl.BlockSpec((tm, c), lambda i: (i, 0))],
        out_specs=pl.BlockSpec((tm, c), lambda i: (i, 0)),
        out_shape=jax.ShapeDtypeStruct((t, c), F32),
        compiler_params=_params(("parallel",), 56),
        name="merge_outproj",
    )(za, za, za, za, za, za, za, y_attn, conv_w, w_out, x)


def _xattn_body(h_ref, g_ref, wq_ref, kv_ref, wo_ref, o_ref):
    h = h_ref[...]
    hn = _rms(h, g_ref[...]).astype(BF16)
    q = jnp.dot(hn, wq_ref[...], preferred_element_type=F32) * (X_HEAD_DIM ** -0.5)
    q = q.astype(BF16)
    kv = kv_ref[...]
    kw = X_HEADS * X_HEAD_DIM
    outs = []
    for hh in range(X_HEADS):
        sl = slice(hh * X_HEAD_DIM, (hh + 1) * X_HEAD_DIM)
        s = lax.dot_general(q[:, sl], kv[:, sl], (((1,), (1,)), ((), ())), preferred_element_type=F32)
        m = jnp.max(s, axis=1, keepdims=True)
        p = jnp.exp(s - m)
        l = jnp.sum(p, axis=1, keepdims=True)
        vh = kv[:, kw + hh * X_HEAD_DIM: kw + (hh + 1) * X_HEAD_DIM]
        outs.append((jnp.dot(p.astype(BF16), vh, preferred_element_type=F32) / l).astype(BF16))
    o = jnp.concatenate(outs, axis=1)
    o_ref[...] = h + jnp.dot(o, wo_ref[...], preferred_element_type=F32)


def cross_attention(h, g, wq, kv, wo, s, tm):
    t, d = h.shape
    m = kv.shape[0] // (t // s)
    kw = X_HEADS * X_HEAD_DIM
    tps = s // tm
    return pl.pallas_call(
        _xattn_body,
        grid=(t // tm,),
        in_specs=[pl.BlockSpec((tm, d), lambda i: (i, 0)),
                  pl.BlockSpec((1, d), lambda i: (0, 0)),
                  pl.BlockSpec((d, kw), lambda i: (0, 0)),
                  pl.BlockSpec((m, 2 * kw), lambda i: (i // tps, 0)),
                  pl.BlockSpec((kw, d), lambda i: (0, 0))],
        out_specs=pl.BlockSpec((tm, d), lambda i: (i, 0)),
        out_shape=jax.ShapeDtypeStruct((t, d), F32),
        compiler_params=_params(("parallel",), 48),
        name="cross_attention",
    )(h, g.reshape(1, d), wq, kv, wo)


def _router_body(h_ref, g_ref, wr_ref, br_ref, hf_ref, idx_ref, gate_ref):
    hf = _rms(h_ref[...], g_ref[...])
    hf_ref[...] = hf.astype(BF16)
    logits = jnp.dot(hf, wr_ref[...], preferred_element_type=F32, precision=lax.Precision.HIGHEST) + br_ref[...]
    lane = lax.broadcasted_iota(I32, logits.shape, 1)
    lane_f = lane.astype(F32)
    l = jnp.where(lane < N_EXPERTS, logits, -jnp.inf)
    vals, idxs = [], []
    for _ in range(TOP_K_EXPERTS):
        m = jnp.max(l, axis=1, keepdims=True)
        ix = jnp.min(jnp.where(l == m, lane_f, float(LANES)), axis=1, keepdims=True)
        vals.append(m)
        idxs.append(ix)
        l = jnp.where(lane_f == ix, -jnp.inf, l)
    es = [jnp.exp(v - vals[0]) for v in vals]
    den = es[0] + es[1] + es[2] + es[3]
    idx_o = jnp.zeros(logits.shape, F32)
    gate_o = jnp.zeros(logits.shape, F32)
    for kk in range(TOP_K_EXPERTS):
        idx_o = jnp.where(lane == kk, idxs[kk], idx_o)
        gate_o = jnp.where(lane == kk, es[kk] / den, gate_o)
    idx_ref[...] = idx_o.astype(I32)
    gate_ref[...] = gate_o


def ffn_router(h, g, w_router, b_router, tm):
    t, d = h.shape
    wr = jnp.zeros((d, LANES), F32).at[:, :N_EXPERTS].set(w_router)
    br = jnp.zeros((1, LANES), F32).at[0, :N_EXPERTS].set(b_router)
    return pl.pallas_call(
        _router_body,
        grid=(t // tm,),
        in_specs=[pl.BlockSpec((tm, d), lambda i: (i, 0)),
                  pl.BlockSpec((1, d), lambda i: (0, 0)),
                  pl.BlockSpec((d, LANES), lambda i: (0, 0)),
                  pl.BlockSpec((1, LANES), lambda i: (0, 0))],
        out_specs=[pl.BlockSpec((tm, d), lambda i: (i, 0)),
                   pl.BlockSpec((tm, LANES), lambda i: (i, 0)),
                   pl.BlockSpec((tm, LANES), lambda i: (i, 0))],
        out_shape=[jax.ShapeDtypeStruct((t, d), BF16),
                   jax.ShapeDtypeStruct((t, LANES), I32),
                   jax.ShapeDtypeStruct((t, LANES), F32)],
        compiler_params=_params(("parallel",), 48),
        name="ffn_router",
    )(h, g.reshape(1, d), wr, br)


def _expert_changed(te_ref, i):
    return (i == 0) | (te_ref[i] != te_ref[jnp.maximum(i - 1, 0)])


def _gmm1_body(te_ref, nu_ref, x_ref, wg_ref, wu_ref, bg_ref, bu_ref, o_ref, wg_s, wu_s):
    i = pl.program_id(1)

    @pl.when(i < nu_ref[0])
    def _():
        @pl.when(_expert_changed(te_ref, i))
        def _():
            wg_s[...] = wg_ref[0].astype(BF16)
            wu_s[...] = wu_ref[0].astype(BF16)

        x = x_ref[...]
        gate = jnp.dot(x, wg_s[...], preferred_element_type=F32) + bg_ref[0]
        up = jnp.dot(x, wu_s[...], preferred_element_type=F32) + bu_ref[0]
        gate = jnp.minimum(gate, SWIGLU_LIMIT)
        up = jnp.clip(up, -SWIGLU_LIMIT, SWIGLU_LIMIT)
        o_ref[...] = ((up + 1.0) * (gate * jax.nn.sigmoid(SWIGLU_ALPHA * gate))).astype(o_ref.dtype)


def _gmm2_body(te_ref, nu_ref, h_ref, wd_ref, bd_ref, o_ref, wd_s):
    i = pl.program_id(1)

    @pl.when(i < nu_ref[0])
    def _():
        @pl.when(_expert_changed(te_ref, i))
        def _():
            wd_s[...] = wd_ref[0].astype(BF16)

        o_ref[...] = (jnp.dot(h_ref[...], wd_s[...], preferred_element_type=F32) + bd_ref[0]).astype(o_ref.dtype)


def expert_ffn(xs, tile_e, n_used, w_gu, b_gu, w_dn, b_dn, tf, tn):
    cap, d = xs.shape
    dff = w_dn.shape[1]
    n_tiles = cap // EXPERT_BLOCK
    nj = dff // tf
    tile = lambda i, nu: jnp.minimum(i, nu[0] - 1)
    b_gu3 = b_gu.reshape(N_EXPERTS, 1, 2 * dff)
    b_dn3 = b_dn.reshape(N_EXPERTS, 1, d)
    hdn = pl.pallas_call(
        _gmm1_body,
        grid_spec=pltpu.PrefetchScalarGridSpec(
            num_scalar_prefetch=2,
            grid=(nj, n_tiles),
            in_specs=[pl.BlockSpec((EXPERT_BLOCK, d), lambda j, i, te, nu: (tile(i, nu), 0)),
                      pl.BlockSpec((1, d, tf), lambda j, i, te, nu: (te[tile(i, nu)], 0, j)),
                      pl.BlockSpec((1, d, tf), lambda j, i, te, nu: (te[tile(i, nu)], 0, nj + j)),
                      pl.BlockSpec((1, 1, tf), lambda j, i, te, nu: (te[tile(i, nu)], 0, j)),
                      pl.BlockSpec((1, 1, tf), lambda j, i, te, nu: (te[tile(i, nu)], 0, nj + j))],
            out_specs=pl.BlockSpec((EXPERT_BLOCK, tf), lambda j, i, te, nu: (tile(i, nu), j)),
            scratch_shapes=[pltpu.VMEM((d, tf), BF16), pltpu.VMEM((d, tf), BF16)]),
        out_shape=jax.ShapeDtypeStruct((cap, dff), BF16),
        compiler_params=_params(("arbitrary", "arbitrary"), 60),
        name="expert_gate_up",
    )(tile_e, n_used, xs, w_gu, w_gu, b_gu3, b_gu3)
    nn = d // tn
    return pl.pallas_call(
        _gmm2_body,
        grid_spec=pltpu.PrefetchScalarGridSpec(
            num_scalar_prefetch=2,
            grid=(nn, n_tiles),
            in_specs=[pl.BlockSpec((EXPERT_BLOCK, dff), lambda j, i, te, nu: (tile(i, nu), 0)),
                      pl.BlockSpec((1, dff, tn), lambda j, i, te, nu: (te[tile(i, nu)], 0, j)),
                      pl.BlockSpec((1, 1, tn), lambda j, i, te, nu: (te[tile(i, nu)], 0, j))],
            out_specs=pl.BlockSpec((EXPERT_BLOCK, tn), lambda j, i, te, nu: (tile(i, nu), j)),
            scratch_shapes=[pltpu.VMEM((dff, tn), BF16)]),
        out_shape=jax.ShapeDtypeStruct((cap, d), BF16),
        compiler_params=_params(("arbitrary", "arbitrary"), 60),
        name="expert_down",
    )(tile_e, n_used, hdn, w_dn, b_dn3)


def _combine_body(h_ref, y0_ref, y1_ref, y2_ref, y3_ref, gate_ref, g_ref, o_ref, *, final):
    acc = h_ref[...]
    gate = gate_ref[...]
    for kk, y_ref in enumerate((y0_ref, y1_ref, y2_ref, y3_ref)):
        acc = acc + gate[:, kk:kk + 1] * y_ref[...].astype(F32)
    o_ref[...] = _rms(acc, g_ref[...]) if final else acc


def combine_norm(h, ys, gates, g, tm, final):
    t, d = h.shape
    return pl.pallas_call(
        functools.partial(_combine_body, final=final),
        grid=(t // tm,),
        in_specs=[pl.BlockSpec((tm, d), lambda i: (i, 0))] * (1 + TOP_K_EXPERTS)
                 + [pl.BlockSpec((tm, LANES), lambda i: (i, 0)),
                    pl.BlockSpec((1, d), lambda i: (0, 0))],
        out_specs=pl.BlockSpec((tm, d), lambda i: (i, 0)),
        out_shape=jax.ShapeDtypeStruct((t, d), F32),
        compiler_params=_params(("parallel",), 48),
        name="combine_norm",
    )(h, *ys, gates, g.reshape(1, d))


def _rope_table(positions, rot_dim, head_dim, active_lanes):
    half = rot_dim // 2
    inv_freq = jnp.float32(ROPE_THETA) ** (-(jnp.arange(half, dtype=F32) * 2.0 / rot_dim))
    ang = positions.astype(F32).reshape(-1)[:, None] * inv_freq
    cos, sin = jnp.cos(ang), jnp.sin(ang)
    t = cos.shape[0]
    one = jnp.ones((t, head_dim - rot_dim), F32)
    zero = jnp.zeros((t, head_dim - rot_dim), F32)
    zh = jnp.zeros((t, half), F32)
    c = jnp.concatenate([cos, cos, one], axis=1)
    s1 = jnp.concatenate([zh, sin, zero], axis=1)
    s2 = jnp.concatenate([-sin, zh, zero], axis=1)
    reps = active_lanes // head_dim

    def widen(a, fill):
        a = jnp.tile(a, (1, reps))
        return jnp.concatenate([a, jnp.full((t, LANES - active_lanes), fill, F32)], axis=1)

    return jnp.concatenate([widen(c, 1.0), widen(s1, 0.0), widen(s2, 0.0)], axis=1)


def _route(idx, t, n_tiles):
    flat_e = idx.reshape(-1)
    onehot = (flat_e[:, None] == jnp.arange(N_EXPERTS, dtype=I32)[None, :]).astype(I32)
    csum = jnp.cumsum(onehot, axis=0)
    rank = jnp.take_along_axis(csum, flat_e[:, None], axis=1)[:, 0] - 1
    counts = csum[-1]
    padded = (counts + EXPERT_BLOCK - 1) // EXPERT_BLOCK * EXPERT_BLOCK
    pend = jnp.cumsum(padded)
    pstart = pend - padded
    dest = pstart[flat_e] + rank
    cap = n_tiles * EXPERT_BLOCK
    buf_tok = jnp.full((cap,), t, I32).at[dest].set(jnp.arange(flat_e.shape[0], dtype=I32) // TOP_K_EXPERTS)
    tile_start = jnp.arange(n_tiles, dtype=I32) * EXPERT_BLOCK
    tile_e = jnp.minimum(jnp.sum((pend[None, :] <= tile_start[:, None]).astype(I32), axis=1), N_EXPERTS - 1)
    n_used = (pend[-1] // EXPERT_BLOCK).astype(I32).reshape(1)
    return dest, buf_tok, tile_e, n_used


def kernel(x, mem, positions, norm_mix, w_in, conv_w, w_out, norm_xattn, norm_mem, wq_x, wk_x, wv_x, wo_x,
           norm_ffn, w_router, b_router, w_gate_up, b_gate_up, w_down, b_down, norm_final):
    b, s, d = x.shape
    t = b * s
    assert d == N_HEADS * HEAD_DIM and s % 512 == 0
    x2 = x.reshape(t, d)
    h = x2
    for l in range(w_in.shape[0]):
        wl = w_in[l]
        o = [0]
        for n in (d, d, d, d, N_KV_HEADS * HEAD_DIM, N_KV_HEADS * HEAD_DIM, IDX_HEADS * IDX_DIM, IDX_DIM, IDX_HEADS, d, d):
            o.append(o[-1] + n)
        seg = lambda a: wl[:, o[a]:o[a + 1]]
        w_a = jnp.concatenate([seg(0), seg(1), seg(2), seg(9), seg(10), seg(5)], axis=1).astype(BF16)
        pad_b = jnp.zeros((d, LANES - IDX_DIM - IDX_HEADS + 128), F32)
        w_b = jnp.concatenate([seg(3), seg(6), seg(4), seg(7), seg(8), pad_b], axis=1).astype(BF16)
        za = norm_matmul(h, norm_mix[l], w_a, BF16, 1024, 512)
        zb = norm_matmul(h, norm_mix[l], w_b, F32, 1024, 768)
        tq = _rope_table(positions, ROT_DIM, HEAD_DIM, LANES)
        ti = _rope_table(positions, IDX_ROT_DIM, IDX_DIM, LANES)
        tk = _rope_table(positions, IDX_ROT_DIM, IDX_DIM, IDX_DIM)
        tq_rows = 256
        q_r, qi_r, k_r, ki_r, wi_r = rope_split(zb, tq, ti, tk, b, s, tq_rows)
        topk = min(TOPK_MAX, s // 4)
        tkc = 512
        wi_t = wi_r.reshape(b, s // Q_BLOCK, Q_BLOCK, IDX_HEADS).transpose(0, 1, 3, 2)
        bias = index_select(qi_r, ki_r, wi_t, b, s, topk, tkc)
        y_attn = masked_attention(q_r, k_r, za, 5 * d // HEAD_DIM, bias, b, s, tq_rows, tkc)
        h = merge_outproj(za, y_attn, conv_w[l], w_out[l].astype(BF16), h, s, 256)
        kv_w = jnp.concatenate([wk_x[l], wv_x[l]], axis=1).astype(BF16)
        kv = norm_matmul(mem.reshape(-1, d), norm_mem[l], kv_w, BF16, 1024, 512)
        h = cross_attention(h, norm_xattn[l], wq_x[l].astype(BF16), kv, wo_x[l].astype(BF16), s, 256)
        hf, idx, gates = ffn_router(h, norm_ffn[l], w_router[l], b_router[l], 256)
        n_tiles = -(-(t * TOP_K_EXPERTS) // EXPERT_BLOCK) + N_EXPERTS
        dest, buf_tok, tile_e, n_used = _route(idx[:, :TOP_K_EXPERTS], t, n_tiles)
        hf_pad = jnp.concatenate([hf, jnp.zeros((1, d), BF16)], axis=0)
        xs = hf_pad[buf_tok]
        y = expert_ffn(xs, tile_e, n_used, w_gate_up[l], b_gate_up[l], w_down[l], b_down[l], 1024, 2048)
        dest2 = dest.reshape(t, TOP_K_EXPERTS)
        ys = [y[dest2[:, kk]] for kk in range(TOP_K_EXPERTS)]
        h = combine_norm(h, ys, gates, norm_final, 256, final=(l + 1 == w_in.shape[0]))
    return h.reshape(b, s, d)
```

```python
import functools

import jax
import jax.numpy as jnp
from jax import lax
from jax.experimental import pallas as pl
from jax.experimental.pallas import tpu as pltpu

F32 = jnp.float32
BF16 = jnp.bfloat16
I32 = jnp.int32

HEAD_DIM = 128
N_HEADS = 16
N_KV_HEADS = 4
GROUP = N_HEADS // N_KV_HEADS
ROT_DIM = 32
IDX_HEADS = 16
IDX_DIM = 64
IDX_ROT_DIM = 16
TOPK_MAX = 256
Q_BLOCK = 128
ROPE_THETA = 500000.0
X_HEADS = 4
X_HEAD_DIM = 128
N_EXPERTS = 32
TOP_K_EXPERTS = 4
SWIGLU_LIMIT = 7.0
SWIGLU_ALPHA = 1.702
EXPERT_BLOCK = 256
NORM_EPS = 1e-6
CONV_WIDTH = 3

LANES = 128
NEG = -1e30
INT_MIN = -2 ** 31
MIB = 1024 * 1024
LOG2E = 1.4426950408889634


def _params(sem, vmem_mib):
    return pltpu.CompilerParams(dimension_semantics=sem, vmem_limit_bytes=vmem_mib * MIB)


def _rms(x, g):
    ms = jnp.mean(x * x, axis=-1, keepdims=True)
    return x * lax.rsqrt(ms + NORM_EPS) * g


def _norm_matmul_body(x_ref, g_ref, w_ref, o_ref, a_scr):
    @pl.when(pl.program_id(1) == 0)
    def _():
        a_scr[...] = _rms(x_ref[...], g_ref[...]).astype(BF16)

    o_ref[...] = jnp.dot(a_scr[...], w_ref[...], preferred_element_type=F32).astype(o_ref.dtype)


def norm_matmul(x, g, w, out_dtype, tm, tn):
    m, k = x.shape
    n = w.shape[1]
    tm = min(tm, m)
    assert m % tm == 0 and n % tn == 0
    return pl.pallas_call(
        _norm_matmul_body,
        grid=(m // tm, n // tn),
        in_specs=[pl.BlockSpec((tm, k), lambda i, j: (i, 0)),
                  pl.BlockSpec((1, k), lambda i, j: (0, 0)),
                  pl.BlockSpec((k, tn), lambda i, j: (0, j))],
        out_specs=pl.BlockSpec((tm, tn), lambda i, j: (i, j)),
        out_shape=jax.ShapeDtypeStruct((m, n), out_dtype),
        scratch_shapes=[pltpu.VMEM((tm, k), BF16)],
        compiler_params=_params(("parallel", "arbitrary"), 48),
        name="norm_matmul",
    )(x, g.reshape(1, k), w)


def _rope(x, tab, shift):
    c, s1, s2 = tab[:, 0:LANES], tab[:, LANES:2 * LANES], tab[:, 2 * LANES:3 * LANES]
    return x * c + pltpu.roll(x, shift, 1) * s1 + pltpu.roll(x, LANES - shift, 1) * s2


def _rope_body(q_ref, qi_ref, k_ref, kw_ref, tq_ref, ti_ref, tk_ref,
               qo_ref, qio_ref, ko_ref, kio_ref, wio_ref, *, tm):
    tq = tq_ref[...]
    scale = HEAD_DIM ** -0.5 * LOG2E
    for h in range(N_HEADS):
        xh = q_ref[:, h * LANES:(h + 1) * LANES]
        xr = (_rope(xh, tq, ROT_DIM // 2) * scale).astype(BF16)
        for qb in range(tm // Q_BLOCK):
            qo_ref[0, qb, h * Q_BLOCK:(h + 1) * Q_BLOCK, :] = xr[qb * Q_BLOCK:(qb + 1) * Q_BLOCK]
    for g in range(N_KV_HEADS):
        xh = k_ref[:, g * LANES:(g + 1) * LANES]
        ko_ref[:, g * LANES:(g + 1) * LANES] = _rope(xh, tq, ROT_DIM // 2).astype(BF16)
    ti = ti_ref[...]
    for j in range(IDX_HEADS * IDX_DIM // LANES):
        xr = _rope(qi_ref[:, j * LANES:(j + 1) * LANES], ti, IDX_ROT_DIM // 2).astype(BF16)
        for half in range(LANES // IDX_DIM):
            h = j * (LANES // IDX_DIM) + half
            part = xr[:, half * IDX_DIM:(half + 1) * IDX_DIM]
            for qb in range(tm // Q_BLOCK):
                qio_ref[0, qb, h * Q_BLOCK:(h + 1) * Q_BLOCK, :] = part[qb * Q_BLOCK:(qb + 1) * Q_BLOCK]
    kw = kw_ref[...]
    kr = _rope(kw, tk_ref[...], IDX_ROT_DIM // 2)
    kio_ref[...] = kr[:, 0:IDX_DIM].astype(BF16)
    wio_ref[...] = kw[:, IDX_DIM:IDX_DIM + IDX_HEADS] * (IDX_HEADS ** -0.5 * IDX_DIM ** -0.5)


def rope_split(zb, tq, ti, tk, b, s, tm):
    t = b * s
    nq = s // tm
    d = N_HEADS * HEAD_DIM
    qi_w = IDX_HEADS * IDX_DIM
    kv_w = N_KV_HEADS * HEAD_DIM
    off_qi = d // qi_w
    off_k = (d + qi_w) // kv_w
    off_kw = (d + qi_w + kv_w) // LANES
    row = lambda bb, i: bb * nq + i
    return pl.pallas_call(
        functools.partial(_rope_body, tm=tm),
        grid=(b, nq),
        in_specs=[pl.BlockSpec((tm, d), lambda bb, i: (row(bb, i), 0)),
                  pl.BlockSpec((tm, qi_w), lambda bb, i: (row(bb, i), off_qi)),
                  pl.BlockSpec((tm, kv_w), lambda bb, i: (row(bb, i), off_k)),
                  pl.BlockSpec((tm, LANES), lambda bb, i: (row(bb, i), off_kw)),
                  pl.BlockSpec((tm, 3 * LANES), lambda bb, i: (row(bb, i), 0)),
                  pl.BlockSpec((tm, 3 * LANES), lambda bb, i: (row(bb, i), 0)),
                  pl.BlockSpec((tm, 3 * LANES), lambda bb, i: (row(bb, i), 0))],
        out_specs=[pl.BlockSpec((1, tm // Q_BLOCK, N_HEADS * Q_BLOCK, HEAD_DIM), lambda bb, i: (bb, i, 0, 0)),
                   pl.BlockSpec((1, tm // Q_BLOCK, IDX_HEADS * Q_BLOCK, IDX_DIM), lambda bb, i: (bb, i, 0, 0)),
                   pl.BlockSpec((tm, kv_w), lambda bb, i: (row(bb, i), 0)),
                   pl.BlockSpec((tm, IDX_DIM), lambda bb, i: (row(bb, i), 0)),
                   pl.BlockSpec((tm, IDX_HEADS), lambda bb, i: (row(bb, i), 0))],
        out_shape=[jax.ShapeDtypeStruct((b, s // Q_BLOCK, N_HEADS * Q_BLOCK, HEAD_DIM), BF16),
                   jax.ShapeDtypeStruct((b, s // Q_BLOCK, IDX_HEADS * Q_BLOCK, IDX_DIM), BF16),
                   jax.ShapeDtypeStruct((t, kv_w), BF16),
                   jax.ShapeDtypeStruct((t, IDX_DIM), BF16),
                   jax.ShapeDtypeStruct((t, IDX_HEADS), F32)],
        compiler_params=_params(("parallel", "parallel"), 48),
        name="rope_split",
    )(zb, zb, zb, zb, tq, ti, tk)


def _index_body(qi_ref, ki_ref, wi_ref, o_ref, skey_ref, *, topk, ch):
    i = pl.program_id(1)
    nch = (i * Q_BLOCK + Q_BLOCK + ch - 1) // ch
    qi = qi_ref[0, 0]
    wi = wi_ref[0, 0]
    key_pos = lax.broadcasted_iota(I32, (ch, Q_BLOCK), 0)
    q_pos = i * Q_BLOCK + lax.broadcasted_iota(I32, (ch, Q_BLOCK), 1)

    def score_chunk(c, carry):
        kc = ki_ref[pl.ds(pl.multiple_of(c * ch, ch), ch), :]
        logits = lax.dot_general(kc, qi, (((1,), (1,)), ((), ())), preferred_element_type=F32)
        acc = jnp.zeros((ch, Q_BLOCK), F32)
        for h in range(IDX_HEADS):
            acc = acc + jnp.maximum(logits[:, h * Q_BLOCK:(h + 1) * Q_BLOCK], 0.0) * wi[h:h + 1, :]
        bits = lax.bitcast_convert_type(acc, I32)
        key = bits ^ ((bits >> 31) & 0x7FFFFFFF)
        skey_ref[c] = jnp.where(c * ch + key_pos <= q_pos, key, INT_MIN)
        return carry

    lax.fori_loop(0, nch, score_chunk, 0)

    def count_ge(cand):
        def body(c, cnt):
            m = jnp.where(skey_ref[c] >= cand, 1.0, 0.0)
            return cnt + jnp.sum(m.reshape(ch // 64, 64, Q_BLOCK), axis=0)
        cnt = lax.fori_loop(0, nch, body, jnp.zeros((64, Q_BLOCK), F32))
        return jnp.sum(cnt, axis=0, keepdims=True)

    def bit_body(t, u):
        cand_u = u | jnp.left_shift(jnp.int32(1), 31 - t)
        cnt = count_ge(cand_u ^ INT_MIN)
        return jnp.where(cnt >= topk, cand_u, u)

    u = lax.fori_loop(0, 32, bit_body, jnp.zeros((1, Q_BLOCK), I32))
    thr = jnp.maximum(u ^ INT_MIN, INT_MIN + 1)

    o_ref[...] = jnp.full(o_ref.shape, NEG, o_ref.dtype)

    def write_chunk(c, carry):
        o_ref[0, 0, c] = jnp.where(skey_ref[c] >= thr, 0.0, NEG).astype(o_ref.dtype)
        return carry

    lax.fori_loop(0, nch, write_chunk, 0)


def index_select(qi, ki, wi_t, b, s, topk, ch):
    nqb = s // Q_BLOCK
    nkc = s // ch
    return pl.pallas_call(
        functools.partial(_index_body, topk=topk, ch=ch),
        grid=(b, nqb),
        in_specs=[pl.BlockSpec((1, 1, IDX_HEADS * Q_BLOCK, IDX_DIM), lambda bb, i: (bb, i, 0, 0)),
                  pl.BlockSpec((s, IDX_DIM), lambda bb, i: (bb, 0)),
                  pl.BlockSpec((1, 1, IDX_HEADS, Q_BLOCK), lambda bb, i: (bb, i, 0, 0))],
        out_specs=pl.BlockSpec((1, 1, nkc, ch, Q_BLOCK), lambda bb, i: (bb, i, 0, 0, 0)),
        out_shape=jax.ShapeDtypeStruct((b, nqb, nkc, ch, Q_BLOCK), BF16),
        scratch_shapes=[pltpu.VMEM((nkc, ch, Q_BLOCK), I32)],
        compiler_params=_params(("parallel", "arbitrary"), 48),
        name="index_select",
    )(qi, ki, wi_t)


def _attn_body(pi_ref, pk_ref, q_ref, k_ref, v_ref, b_ref, o_ref, lhs_scr, m_scr, acc_scr, *, tq, tk):
    step = pl.program_id(2)
    i = pi_ref[step]
    kc = pk_ref[step]
    last = ((i + 1) * tq - 1) // tk
    nqb = tq // Q_BLOCK
    rows = GROUP * Q_BLOCK

    @pl.when(kc == 0)
    def _():
        r = lax.broadcasted_iota(I32, (rows, Q_BLOCK), 0)
        c = lax.broadcasted_iota(I32, (rows, Q_BLOCK), 1)
        eye = jnp.where(r % Q_BLOCK == c, 1.0, 0.0).astype(BF16)
        for j in range(nqb):
            lhs_scr[j] = jnp.concatenate([q_ref[0, j], eye], axis=1)
        m_scr[...] = jnp.full(m_scr.shape, NEG, F32)
        acc_scr[...] = jnp.zeros(acc_scr.shape, F32)

    k = k_ref[...]
    v_one = jnp.concatenate([v_ref[...], jnp.ones((tk, LANES), BF16)], axis=1)
    for j in range(nqb):
        rhs = jnp.concatenate([k, b_ref[0, j, 0]], axis=1)
        s = lax.dot_general(lhs_scr[j], rhs, (((1,), (1,)), ((), ())), preferred_element_type=F32)
        m_prev = m_scr[j]
        m_new = jnp.maximum(m_prev, jnp.max(s, axis=1, keepdims=True))
        alpha = jnp.exp2(m_prev - m_new)
        p = jnp.exp2(s - jnp.concatenate([m_new] * (tk // LANES), axis=1))
        acc_scr[j] = (jnp.concatenate([alpha, alpha], axis=1) * acc_scr[j]
                      + jnp.dot(p.astype(BF16), v_one, preferred_element_type=F32))
        m_scr[j] = m_new

    @pl.when(kc == last)
    def _():
        for j in range(nqb):
            acc = acc_scr[j]
            o = acc[:, 0:HEAD_DIM] / acc[:, HEAD_DIM:2 * HEAD_DIM]
            for h in range(GROUP):
                o_ref[j * Q_BLOCK:(j + 1) * Q_BLOCK, h * HEAD_DIM:(h + 1) * HEAD_DIM] = (
                    o[h * Q_BLOCK:(h + 1) * Q_BLOCK].astype(o_ref.dtype))


def masked_attention(q, k, za, v_col0, bias, b, s, tq, tk):
    nqt = s // tq
    nkc = s // tk
    t = b * s
    pairs = [(i, kc) for i in range(nqt) for kc in range(((i + 1) * tq - 1) // tk + 1)]
    pi = jnp.asarray([p[0] for p in pairs], I32)
    pk = jnp.asarray([p[1] for p in pairs], I32)
    return pl.pallas_call(
        functools.partial(_attn_body, tq=tq, tk=tk),
        grid_spec=pltpu.PrefetchScalarGridSpec(
            num_scalar_prefetch=2,
            grid=(b, N_KV_HEADS, len(pairs)),
            in_specs=[pl.BlockSpec((1, tq // Q_BLOCK, GROUP * Q_BLOCK, HEAD_DIM),
                                   lambda bb, g, p, pi, pk: (bb, pi[p], g, 0)),
                      pl.BlockSpec((tk, HEAD_DIM), lambda bb, g, p, pi, pk: (bb * nkc + pk[p], g)),
                      pl.BlockSpec((tk, HEAD_DIM), lambda bb, g, p, pi, pk: (bb * nkc + pk[p], v_col0 + g)),
                      pl.BlockSpec((1, tq // Q_BLOCK, 1, tk, Q_BLOCK),
                                   lambda bb, g, p, pi, pk: (bb, pi[p], pk[p], 0, 0))],
            out_specs=pl.BlockSpec((tq, GROUP * HEAD_DIM), lambda bb, g, p, pi, pk: (bb * nqt + pi[p], g)),
            scratch_shapes=[pltpu.VMEM((tq // Q_BLOCK, GROUP * Q_BLOCK, 2 * HEAD_DIM), BF16),
                            pltpu.VMEM((tq // Q_BLOCK, GROUP * Q_BLOCK, LANES), F32),
                            pltpu.VMEM((tq // Q_BLOCK, GROUP * Q_BLOCK, 2 * HEAD_DIM), F32)]),
        out_shape=jax.ShapeDtypeStruct((t, N_HEADS * HEAD_DIM), BF16),
        compiler_params=_params(("parallel", "parallel", "arbitrary"), 48),
        name="masked_attention",
    )(pi, pk, q, k, za, bias)


def _merge_body(cb_ref, cc_ref, cx_ref, gc_ref, ga_ref, cch_ref, cxh_ref, ya_ref, cw_ref, wo_ref, x_ref,
                o_ref, *, tm, tiles_per_seq):
    i = pl.program_id(0)
    p = cc_ref[...].astype(F32) * cx_ref[...].astype(F32)
    halo = cch_ref[...].astype(F32) * cxh_ref[...].astype(F32)
    halo = jnp.where(i % tiles_per_seq == 0, 0.0, halo)
    row = lax.broadcasted_iota(I32, p.shape, 0)
    p1 = jnp.where(row == 0, halo[7:8], pltpu.roll(p, 1, 0))
    p2 = jnp.where(row == 0, halo[6:7], jnp.where(row == 1, halo[7:8], pltpu.roll(p, 2, 0)))
    cw = cw_ref[...]
    conv = cw[0:1] * p2 + cw[1:2] * p1 + cw[2:3] * p
    merged = (jax.nn.sigmoid(gc_ref[...].astype(F32)) * (cb_ref[...].astype(F32) * conv)
              + jax.nn.sigmoid(ga_ref[...].astype(F32)) * ya_ref[...].astype(F32))
    o_ref[...] = x_ref[...] + jnp.dot(merged.astype(BF16), wo_ref[...], preferred_element_type=F32)


def merge_outproj(za, y_attn, conv_w, w_out, x, s, tm):
    t, c = x.shape
    hb = tm // 8
    col = lambda j: (lambda i: (i, j))
    halo = lambda j: (lambda i: (jnp.maximum(i * hb - 1, 0), j))
    return pl.pallas_call(
        functools.partial(_merge_body, tm=tm, tiles_per_seq=s // tm),
        grid=(t // tm,),
        in_specs=[pl.BlockSpec((tm, c), col(0)), pl.BlockSpec((tm, c), col(1)), pl.BlockSpec((tm, c), col(2)),
                  pl.BlockSpec((tm, c), col(3)), pl.BlockSpec((tm, c), col(4)),
                  pl.BlockSpec((8, c), halo(1)), pl.BlockSpec((8, c), halo(2)),
                  pl.BlockSpec((tm, c), lambda i: (i, 0)),
                  pl.BlockSpec((CONV_WIDTH, c), lambda i: (0, 0)),
                  pl.BlockSpec((c, c), lambda i: (0, 0)),
                  pl.BlockSpec((tm, c), lambda i: (i, 0))],
        out_specs=pl.BlockSpec((tm, c), lambda i: (i, 0)),
        out_shape=jax.ShapeDtypeStruct((t, c), F32),
        compiler_params=_params(("parallel",), 56),
        name="merge_outproj",
    )(za, za, za, za, za, za, za, y_attn, conv_w, w_out, x)


def _xattn_body(h_ref, g_ref, wq_ref, kv_ref, wo_ref, o_ref):
    h = h_ref[...]
    hn = _rms(h, g_ref[...]).astype(BF16)
    q = jnp.dot(hn, wq_ref[...], preferred_element_type=F32) * (X_HEAD_DIM ** -0.5)
    q = q.astype(BF16)
    kv = kv_ref[...]
    kw = X_HEADS * X_HEAD_DIM
    outs = []
    for hh in range(X_HEADS):
        sl = slice(hh * X_HEAD_DIM, (hh + 1) * X_HEAD_DIM)
        s = lax.dot_general(q[:, sl], kv[:, sl], (((1,), (1,)), ((), ())), preferred_element_type=F32)
        m = jnp.max(s, axis=1, keepdims=True)
        p = jnp.exp(s - m)
        l = jnp.sum(p, axis=1, keepdims=True)
        vh = kv[:, kw + hh * X_HEAD_DIM: kw + (hh + 1) * X_HEAD_DIM]
        outs.append((jnp.dot(p.astype(BF16), vh, preferred_element_type=F32) / l).astype(BF16))
    o = jnp.concatenate(outs, axis=1)
    o_ref[...] = h + jnp.dot(o, wo_ref[...], preferred_element_type=F32)


def cross_attention(h, g, wq, kv, wo, s, tm):
    t, d = h.shape
    m = kv.shape[0] // (t // s)
    kw = X_HEADS * X_HEAD_DIM
    tps = s // tm
    return pl.pallas_call(
        _xattn_body,
        grid=(t // tm,),
        in_specs=[pl.BlockSpec((tm, d), lambda i: (i, 0)),
                  pl.BlockSpec((1, d), lambda i: (0, 0)),
                  pl.BlockSpec((d, kw), lambda i: (0, 0)),
                  pl.BlockSpec((m, 2 * kw), lambda i: (i // tps, 0)),
                  pl.BlockSpec((kw, d), lambda i: (0, 0))],
        out_specs=pl.BlockSpec((tm, d), lambda i: (i, 0)),
        out_shape=jax.ShapeDtypeStruct((t, d), F32),
        compiler_params=_params(("parallel",), 48),
        name="cross_attention",
    )(h, g.reshape(1, d), wq, kv, wo)


def _router_body(h_ref, g_ref, wr_ref, br_ref, hf_ref, idx_ref, gate_ref):
    hf = _rms(h_ref[...], g_ref[...])
    hf_ref[...] = hf.astype(BF16)
    logits = jnp.dot(hf, wr_ref[...], preferred_element_type=F32, precision=lax.Precision.HIGHEST) + br_ref[...]
    lane = lax.broadcasted_iota(I32, logits.shape, 1)
    lane_f = lane.astype(F32)
    l = jnp.where(lane < N_EXPERTS, logits, -jnp.inf)
    vals, idxs = [], []
    for _ in range(TOP_K_EXPERTS):
        m = jnp.max(l, axis=1, keepdims=True)
        ix = jnp.min(jnp.where(l == m, lane_f, float(LANES)), axis=1, keepdims=True)
        vals.append(m)
        idxs.append(ix)
        l = jnp.where(lane_f == ix, -jnp.inf, l)
    es = [jnp.exp(v - vals[0]) for v in vals]
    den = es[0] + es[1] + es[2] + es[3]
    idx_o = jnp.zeros(logits.shape, F32)
    gate_o = jnp.zeros(logits.shape, F32)
    for kk in range(TOP_K_EXPERTS):
        idx_o = jnp.where(lane == kk, idxs[kk], idx_o)
        gate_o = jnp.where(lane == kk, es[kk] / den, gate_o)
    idx_ref[...] = idx_o.astype(I32)
    gate_ref[...] = gate_o


def ffn_router(h, g, w_router, b_router, tm):
    t, d = h.shape
    wr = jnp.zeros((d, LANES), F32).at[:, :N_EXPERTS].set(w_router)
    br = jnp.zeros((1, LANES), F32).at[0, :N_EXPERTS].set(b_router)
    return pl.pallas_call(
        _router_body,
        grid=(t // tm,),
        in_specs=[pl.BlockSpec((tm, d), lambda i: (i, 0)),
                  pl.BlockSpec((1, d), lambda i: (0, 0)),
                  pl.BlockSpec((d, LANES), lambda i: (0, 0)),
                  pl.BlockSpec((1, LANES), lambda i: (0, 0))],
        out_specs=[pl.BlockSpec((tm, d), lambda i: (i, 0)),
                   pl.BlockSpec((tm, LANES), lambda i: (i, 0)),
                   pl.BlockSpec((tm, LANES), lambda i: (i, 0))],
        out_shape=[jax.ShapeDtypeStruct((t, d), BF16),
                   jax.ShapeDtypeStruct((t, LANES), I32),
                   jax.ShapeDtypeStruct((t, LANES), F32)],
        compiler_params=_params(("parallel",), 48),
        name="ffn_router",
    )(h, g.reshape(1, d), wr, br)


def _expert_changed(te_ref, i):
    return (i == 0) | (te_ref[i] != te_ref[jnp.maximum(i - 1, 0)])


def _gmm1_body(te_ref, nu_ref, x_ref, wg_ref, wu_ref, bg_ref, bu_ref, o_ref, wg_s, wu_s):
    i = pl.program_id(1)

    @pl.when(i < nu_ref[0])
    def _():
        @pl.when(_expert_changed(te_ref, i))
        def _():
            wg_s[...] = wg_ref[0].astype(BF16)
            wu_s[...] = wu_ref[0].astype(BF16)

        x = x_ref[...]
        gate = jnp.dot(x, wg_s[...], preferred_element_type=F32) + bg_ref[0]
        up = jnp.dot(x, wu_s[...], preferred_element_type=F32) + bu_ref[0]
        gate = jnp.minimum(gate, SWIGLU_LIMIT)
        up = jnp.clip(up, -SWIGLU_LIMIT, SWIGLU_LIMIT)
        o_ref[...] = ((up + 1.0) * (gate * jax.nn.sigmoid(SWIGLU_ALPHA * gate))).astype(o_ref.dtype)


def _gmm2_body(te_ref, nu_ref, h_ref, wd_ref, bd_ref, o_ref, wd_s):
    i = pl.program_id(1)

    @pl.when(i < nu_ref[0])
    def _():
        @pl.when(_expert_changed(te_ref, i))
        def _():
            wd_s[...] = wd_ref[0].astype(BF16)

        o_ref[...] = (jnp.dot(h_ref[...], wd_s[...], preferred_element_type=F32) + bd_ref[0]).astype(o_ref.dtype)


def expert_ffn(xs, tile_e, n_used, w_gu, b_gu, w_dn, b_dn, tf, tn):
    cap, d = xs.shape
    dff = w_dn.shape[1]
    n_tiles = cap // EXPERT_BLOCK
    nj = dff // tf
    tile = lambda i, nu: jnp.minimum(i, nu[0] - 1)
    b_gu3 = b_gu.reshape(N_EXPERTS, 1, 2 * dff)
    b_dn3 = b_dn.reshape(N_EXPERTS, 1, d)
    hdn = pl.pallas_call(
        _gmm1_body,
        grid_spec=pltpu.PrefetchScalarGridSpec(
            num_scalar_prefetch=2,
            grid=(nj, n_tiles),
            in_specs=[pl.BlockSpec((EXPERT_BLOCK, d), lambda j, i, te, nu: (tile(i, nu), 0)),
                      pl.BlockSpec((1, d, tf), lambda j, i, te, nu: (te[tile(i, nu)], 0, j)),
                      pl.BlockSpec((1, d, tf), lambda j, i, te, nu: (te[tile(i, nu)], 0, nj + j)),
                      pl.BlockSpec((1, 1, tf), lambda j, i, te, nu: (te[tile(i, nu)], 0, j)),
                      pl.BlockSpec((1, 1, tf), lambda j, i, te, nu: (te[tile(i, nu)], 0, nj + j))],
            out_specs=pl.BlockSpec((EXPERT_BLOCK, tf), lambda j, i, te, nu: (tile(i, nu), j)),
            scratch_shapes=[pltpu.VMEM((d, tf), BF16), pltpu.VMEM((d, tf), BF16)]),
        out_shape=jax.ShapeDtypeStruct((cap, dff), BF16),
        compiler_params=_params(("arbitrary", "arbitrary"), 60),
        name="expert_gate_up",
    )(tile_e, n_used, xs, w_gu, w_gu, b_gu3, b_gu3)
    nn = d // tn
    return pl.pallas_call(
        _gmm2_body,
        grid_spec=pltpu.PrefetchScalarGridSpec(
            num_scalar_prefetch=2,
            grid=(nn, n_tiles),
            in_specs=[pl.BlockSpec((EXPERT_BLOCK, dff), lambda j, i, te, nu: (tile(i, nu), 0)),
                      pl.BlockSpec((1, dff, tn), lambda j, i, te, nu: (te[tile(i, nu)], 0, j)),
                      pl.BlockSpec((1, 1, tn), lambda j, i, te, nu: (te[tile(i, nu)], 0, j))],
            out_specs=pl.BlockSpec((EXPERT_BLOCK, tn), lambda j, i, te, nu: (tile(i, nu), j)),
            scratch_shapes=[pltpu.VMEM((dff, tn), BF16)]),
        out_shape=jax.ShapeDtypeStruct((cap, d), BF16),
        compiler_params=_params(("arbitrary", "arbitrary"), 60),
        name="expert_down",
    )(tile_e, n_used, hdn, w_dn, b_dn3)


def _combine_body(h_ref, y0_ref, y1_ref, y2_ref, y3_ref, gate_ref, g_ref, o_ref, *, final):
    acc = h_ref[...]
    gate = gate_ref[...]
    for kk, y_ref in enumerate((y0_ref, y1_ref, y2_ref, y3_ref)):
        acc = acc + gate[:, kk:kk + 1] * y_ref[...].astype(F32)
    o_ref[...] = _rms(acc, g_ref[...]) if final else acc


def combine_norm(h, ys, gates, g, tm, final):
    t, d = h.shape
    return pl.pallas_call(
        functools.partial(_combine_body, final=final),
        grid=(t // tm,),
        in_specs=[pl.BlockSpec((tm, d), lambda i: (i, 0))] * (1 + TOP_K_EXPERTS)
                 + [pl.BlockSpec((tm, LANES), lambda i: (i, 0)),
                    pl.BlockSpec((1, d), lambda i: (0, 0))],
        out_specs=pl.BlockSpec((tm, d), lambda i: (i, 0)),
        out_shape=jax.ShapeDtypeStruct((t, d), F32),
        compiler_params=_params(("parallel",), 48),
        name="combine_norm",
    )(h, *ys, gates, g.reshape(1, d))


def _rope_table(positions, rot_dim, head_dim, active_lanes):
    half = rot_dim // 2
    inv_freq = jnp.float32(ROPE_THETA) ** (-(jnp.arange(half, dtype=F32) * 2.0 / rot_dim))
    ang = positions.astype(F32).reshape(-1)[:, None] * inv_freq
    cos, sin = jnp.cos(ang), jnp.sin(ang)
    t = cos.shape[0]
    one = jnp.ones((t, head_dim - rot_dim), F32)
    zero = jnp.zeros((t, head_dim - rot_dim), F32)
    zh = jnp.zeros((t, half), F32)
    c = jnp.concatenate([cos, cos, one], axis=1)
    s1 = jnp.concatenate([zh, sin, zero], axis=1)
    s2 = jnp.concatenate([-sin, zh, zero], axis=1)
    reps = active_lanes // head_dim

    def widen(a, fill):
        a = jnp.tile(a, (1, reps))
        return jnp.concatenate([a, jnp.full((t, LANES - active_lanes), fill, F32)], axis=1)

    return jnp.concatenate([widen(c, 1.0), widen(s1, 0.0), widen(s2, 0.0)], axis=1)


def _route(idx, t, n_tiles):
    flat_e = idx.reshape(-1)
    onehot = (flat_e[:, None] == jnp.arange(N_EXPERTS, dtype=I32)[None, :]).astype(I32)
    csum = jnp.cumsum(onehot, axis=0)
    rank = jnp.take_along_axis(csum, flat_e[:, None], axis=1)[:, 0] - 1
    counts = csum[-1]
    padded = (counts + EXPERT_BLOCK - 1) // EXPERT_BLOCK * EXPERT_BLOCK
    pend = jnp.cumsum(padded)
    pstart = pend - padded
    dest = pstart[flat_e] + rank
    cap = n_tiles * EXPERT_BLOCK
    buf_tok = (jnp.arange(cap, dtype=I32) % t).at[dest].set(jnp.arange(flat_e.shape[0], dtype=I32) // TOP_K_EXPERTS)
    tile_start = jnp.arange(n_tiles, dtype=I32) * EXPERT_BLOCK
    tile_e = jnp.minimum(jnp.sum((pend[None, :] <= tile_start[:, None]).astype(I32), axis=1), N_EXPERTS - 1)
    n_used = (pend[-1] // EXPERT_BLOCK).astype(I32).reshape(1)
    return dest, buf_tok, tile_e, n_used


def kernel(x, mem, positions, norm_mix, w_in, conv_w, w_out, norm_xattn, norm_mem, wq_x, wk_x, wv_x, wo_x,
           norm_ffn, w_router, b_router, w_gate_up, b_gate_up, w_down, b_down, norm_final):
    b, s, d = x.shape
    t = b * s
    assert d == N_HEADS * HEAD_DIM and s % 512 == 0
    x2 = x.reshape(t, d)
    h = x2
    for l in range(w_in.shape[0]):
        wl = w_in[l]
        o = [0]
        for n in (d, d, d, d, N_KV_HEADS * HEAD_DIM, N_KV_HEADS * HEAD_DIM, IDX_HEADS * IDX_DIM, IDX_DIM, IDX_HEADS, d, d):
            o.append(o[-1] + n)
        seg = lambda a: wl[:, o[a]:o[a + 1]]
        w_a = jnp.concatenate([seg(0), seg(1), seg(2), seg(9), seg(10), seg(5)], axis=1).astype(BF16)
        pad_b = jnp.zeros((d, LANES - IDX_DIM - IDX_HEADS + 128), F32)
        w_b = jnp.concatenate([seg(3), seg(6), seg(4), seg(7), seg(8), pad_b], axis=1).astype(BF16)
        za = norm_matmul(h, norm_mix[l], w_a, BF16, 1024, 512)
        zb = norm_matmul(h, norm_mix[l], w_b, F32, 1024, 768)
        tq = _rope_table(positions, ROT_DIM, HEAD_DIM, LANES)
        ti = _rope_table(positions, IDX_ROT_DIM, IDX_DIM, LANES)
        tk = _rope_table(positions, IDX_ROT_DIM, IDX_DIM, IDX_DIM)
        tq_rows = 512
        q_r, qi_r, k_r, ki_r, wi_r = rope_split(zb, tq, ti, tk, b, s, tq_rows)
        topk = min(TOPK_MAX, s // 4)
        tkc = 512
        wi_t = wi_r.reshape(b, s // Q_BLOCK, Q_BLOCK, IDX_HEADS).transpose(0, 1, 3, 2)
        bias = index_select(qi_r, ki_r, wi_t, b, s, topk, tkc)
        y_attn = masked_attention(q_r, k_r, za, 5 * d // HEAD_DIM, bias, b, s, tq_rows, tkc)
        h = merge_outproj(za, y_attn, conv_w[l], w_out[l].astype(BF16), h, s, 256)
        kv_w = jnp.concatenate([wk_x[l], wv_x[l]], axis=1).astype(BF16)
        kv = norm_matmul(mem.reshape(-1, d), norm_mem[l], kv_w, BF16, 1024, 512)
        h = cross_attention(h, norm_xattn[l], wq_x[l].astype(BF16), kv, wo_x[l].astype(BF16), s, 256)
        hf, idx, gates = ffn_router(h, norm_ffn[l], w_router[l], b_router[l], 256)
        n_tiles = -(-(t * TOP_K_EXPERTS) // EXPERT_BLOCK) + N_EXPERTS
        dest, buf_tok, tile_e, n_used = _route(idx[:, :TOP_K_EXPERTS], t, n_tiles)
        xs = hf[buf_tok]
        y = expert_ffn(xs, tile_e, n_used, w_gate_up[l], b_gate_up[l], w_down[l], b_down[l], 1024, 2048)
        dest2 = dest.reshape(t, TOP_K_EXPERTS)
        ys = [y[dest2[:, kk]] for kk in range(TOP_K_EXPERTS)]
        h = combine_norm(h, ys, gates, norm_final, 256, final=(l + 1 == w_in.shape[0]))
    return h.reshape(b, s, d)
```

```python
import functools

import jax
import jax.numpy as jnp
from jax import lax
from jax.experimental import pallas as pl
from jax.experimental.pallas import tpu as pltpu

F32 = jnp.float32
BF16 = jnp.bfloat16
I32 = jnp.int32

HEAD_DIM = 128
N_HEADS = 16
N_KV_HEADS = 4
GROUP = N_HEADS // N_KV_HEADS
ROT_DIM = 32
IDX_HEADS = 16
IDX_DIM = 64
IDX_ROT_DIM = 16
TOPK_MAX = 256
Q_BLOCK = 128
ROPE_THETA = 500000.0
X_HEADS = 4
X_HEAD_DIM = 128
N_EXPERTS = 32
TOP_K_EXPERTS = 4
SWIGLU_LIMIT = 7.0
SWIGLU_ALPHA = 1.702
EXPERT_BLOCK = 256
NORM_EPS = 1e-6
CONV_WIDTH = 3

LANES = 128
NEG = -1e30
INT_MIN = -2 ** 31
MIB = 1024 * 1024
LOG2E = 1.4426950408889634


def _params(sem, vmem_mib):
    return pltpu.CompilerParams(dimension_semantics=sem, vmem_limit_bytes=vmem_mib * MIB)


def _rms(x, g):
    ms = jnp.mean(x * x, axis=-1, keepdims=True)
    return x * lax.rsqrt(ms + NORM_EPS) * g


def _norm_matmul_body(x_ref, g_ref, w_ref, o_ref, a_scr):
    @pl.when(pl.program_id(1) == 0)
    def _():
        a_scr[...] = _rms(x_ref[...], g_ref[...]).astype(BF16)

    o_ref[...] = jnp.dot(a_scr[...], w_ref[...], preferred_element_type=F32).astype(o_ref.dtype)


def norm_matmul(x, g, w, out_dtype, tm, tn):
    m, k = x.shape
    n = w.shape[1]
    tm = min(tm, m)
    assert m % tm == 0 and n % tn == 0
    return pl.pallas_call(
        _norm_matmul_body,
        grid=(m // tm, n // tn),
        in_specs=[pl.BlockSpec((tm, k), lambda i, j: (i, 0)),
                  pl.BlockSpec((1, k), lambda i, j: (0, 0)),
                  pl.BlockSpec((k, tn), lambda i, j: (0, j))],
        out_specs=pl.BlockSpec((tm, tn), lambda i, j: (i, j)),
        out_shape=jax.ShapeDtypeStruct((m, n), out_dtype),
        scratch_shapes=[pltpu.VMEM((tm, k), BF16)],
        compiler_params=_params(("parallel", "arbitrary"), 48),
        name="norm_matmul",
    )(x, g.reshape(1, k), w)


def _rope(x, tab, shift):
    c, s1, s2 = tab[:, 0:LANES], tab[:, LANES:2 * LANES], tab[:, 2 * LANES:3 * LANES]
    return x * c + pltpu.roll(x, shift, 1) * s1 + pltpu.roll(x, LANES - shift, 1) * s2


def _rope_body(q_ref, qi_ref, k_ref, kw_ref, tq_ref, ti_ref, tk_ref,
               qo_ref, qio_ref, ko_ref, kio_ref, wio_ref, *, tm):
    tq = tq_ref[...]
    scale = HEAD_DIM ** -0.5 * LOG2E
    for h in range(N_HEADS):
        xh = q_ref[:, h * LANES:(h + 1) * LANES]
        xr = (_rope(xh, tq, ROT_DIM // 2) * scale).astype(BF16)
        for qb in range(tm // Q_BLOCK):
            qo_ref[0, qb, h * Q_BLOCK:(h + 1) * Q_BLOCK, :] = xr[qb * Q_BLOCK:(qb + 1) * Q_BLOCK]
    for g in range(N_KV_HEADS):
        xh = k_ref[:, g * LANES:(g + 1) * LANES]
        ko_ref[:, g * LANES:(g + 1) * LANES] = _rope(xh, tq, ROT_DIM // 2).astype(BF16)
    ti = ti_ref[...]
    for j in range(IDX_HEADS * IDX_DIM // LANES):
        xr = _rope(qi_ref[:, j * LANES:(j + 1) * LANES], ti, IDX_ROT_DIM // 2).astype(BF16)
        for half in range(LANES // IDX_DIM):
            h = j * (LANES // IDX_DIM) + half
            part = xr[:, half * IDX_DIM:(half + 1) * IDX_DIM]
            for qb in range(tm // Q_BLOCK):
                qio_ref[0, qb, h * Q_BLOCK:(h + 1) * Q_BLOCK, :] = part[qb * Q_BLOCK:(qb + 1) * Q_BLOCK]
    kw = kw_ref[...]
    kr = _rope(kw, tk_ref[...], IDX_ROT_DIM // 2)
    kio_ref[...] = kr[:, 0:IDX_DIM].astype(BF16)
    wio_ref[...] = kw[:, IDX_DIM:IDX_DIM + IDX_HEADS] * (IDX_HEADS ** -0.5 * IDX_DIM ** -0.5)


def rope_split(zb, tq, ti, tk, b, s, tm):
    t = b * s
    nq = s // tm
    d = N_HEADS * HEAD_DIM
    qi_w = IDX_HEADS * IDX_DIM
    kv_w = N_KV_HEADS * HEAD_DIM
    off_qi = d // qi_w
    off_k = (d + qi_w) // kv_w
    off_kw = (d + qi_w + kv_w) // LANES
    row = lambda bb, i: bb * nq + i
    return pl.pallas_call(
        functools.partial(_rope_body, tm=tm),
        grid=(b, nq),
        in_specs=[pl.BlockSpec((tm, d), lambda bb, i: (row(bb, i), 0)),
                  pl.BlockSpec((tm, qi_w), lambda bb, i: (row(bb, i), off_qi)),
                  pl.BlockSpec((tm, kv_w), lambda bb, i: (row(bb, i), off_k)),
                  pl.BlockSpec((tm, LANES), lambda bb, i: (row(bb, i), off_kw)),
                  pl.BlockSpec((tm, 3 * LANES), lambda bb, i: (row(bb, i), 0)),
                  pl.BlockSpec((tm, 3 * LANES), lambda bb, i: (row(bb, i), 0)),
                  pl.BlockSpec((tm, 3 * LANES), lambda bb, i: (row(bb, i), 0))],
        out_specs=[pl.BlockSpec((1, tm // Q_BLOCK, N_HEADS * Q_BLOCK, HEAD_DIM), lambda bb, i: (bb, i, 0, 0)),
                   pl.BlockSpec((1, tm // Q_BLOCK, IDX_HEADS * Q_BLOCK, IDX_DIM), lambda bb, i: (bb, i, 0, 0)),
                   pl.BlockSpec((tm, kv_w), lambda bb, i: (row(bb, i), 0)),
                   pl.BlockSpec((tm, IDX_DIM), lambda bb, i: (row(bb, i), 0)),
                   pl.BlockSpec((tm, IDX_HEADS), lambda bb, i: (row(bb, i), 0))],
        out_shape=[jax.ShapeDtypeStruct((b, s // Q_BLOCK, N_HEADS * Q_BLOCK, HEAD_DIM), BF16),
                   jax.ShapeDtypeStruct((b, s // Q_BLOCK, IDX_HEADS * Q_BLOCK, IDX_DIM), BF16),
                   jax.ShapeDtypeStruct((t, kv_w), BF16),
                   jax.ShapeDtypeStruct((t, IDX_DIM), BF16),
                   jax.ShapeDtypeStruct((t, IDX_HEADS), F32)],
        compiler_params=_params(("parallel", "parallel"), 48),
        name="rope_split",
    )(zb, zb, zb, zb, tq, ti, tk)


def _index_body(qi_ref, ki_ref, wi_ref, o_ref, skey_ref, tie_ref, *, topk, ch):
    i = pl.program_id(1)
    nkc = skey_ref.shape[0]
    pos_bits = (nkc * ch).bit_length()
    nch = (i * Q_BLOCK + Q_BLOCK + ch - 1) // ch
    qi = qi_ref[0, 0]
    wi = wi_ref[0, 0]
    key_pos = lax.broadcasted_iota(I32, (ch, Q_BLOCK), 0)
    q_pos = i * Q_BLOCK + lax.broadcasted_iota(I32, (ch, Q_BLOCK), 1)

    def score_chunk(c, carry):
        kc = ki_ref[pl.ds(pl.multiple_of(c * ch, ch), ch), :]
        logits = lax.dot_general(kc, qi, (((1,), (1,)), ((), ())), preferred_element_type=F32)
        acc = jnp.zeros((ch, Q_BLOCK), F32)
        for h in range(IDX_HEADS):
            acc = acc + jnp.maximum(logits[:, h * Q_BLOCK:(h + 1) * Q_BLOCK], 0.0) * wi[h:h + 1, :]
        bits = lax.bitcast_convert_type(acc, I32)
        key = bits ^ ((bits >> 31) & 0x7FFFFFFF)
        skey_ref[c] = jnp.where(c * ch + key_pos <= q_pos, key, INT_MIN)
        return carry

    lax.fori_loop(0, nch, score_chunk, 0)

    def count(pred):
        def body(c, cnt):
            m = jnp.where(pred(c, skey_ref[c]), 1.0, 0.0)
            return cnt + jnp.sum(m.reshape(ch // 64, 64, Q_BLOCK), axis=0)
        cnt = lax.fori_loop(0, nch, body, jnp.zeros((64, Q_BLOCK), F32))
        return jnp.sum(cnt, axis=0, keepdims=True)

    def bit_body(t, u):
        cand_u = u | jnp.left_shift(jnp.int32(1), 31 - t)
        cand = cand_u ^ INT_MIN
        cnt = count(lambda c, keys: keys >= cand)
        return jnp.where(cnt >= topk, cand_u, u)

    u = lax.fori_loop(0, 32, bit_body, jnp.zeros((1, Q_BLOCK), I32))
    thr = jnp.maximum(u ^ INT_MIN, INT_MIN + 1)

    n_ge = count(lambda c, keys: keys >= thr)
    tie_ref[...] = jnp.full(tie_ref.shape, nkc * ch, I32)

    @pl.when(jnp.max(n_ge) > topk)
    def _():
        need = topk - count(lambda c, keys: keys > thr)

        def pos_body(t, end):
            cand = end + jnp.left_shift(jnp.int32(1), pos_bits - 1 - t)
            cnt = count(lambda c, keys: (keys == thr) & (c * ch + key_pos < cand))
            return jnp.where(cnt <= need, cand, end)

        end = lax.fori_loop(0, pos_bits, pos_body, jnp.zeros((1, Q_BLOCK), I32))
        tie_ref[...] = jnp.broadcast_to(jnp.where(n_ge > topk, end, nkc * ch), tie_ref.shape)

    tie_end = tie_ref[0:1, :]
    o_ref[...] = jnp.full(o_ref.shape, NEG, o_ref.dtype)

    def write_chunk(c, carry):
        keys = skey_ref[c]
        keep = (keys > thr) | ((keys == thr) & (c * ch + key_pos < tie_end))
        o_ref[0, 0, c] = jnp.where(keep, 0.0, NEG).astype(o_ref.dtype)
        return carry

    lax.fori_loop(0, nch, write_chunk, 0)


def index_select(qi, ki, wi_t, b, s, topk, ch):
    nqb = s // Q_BLOCK
    nkc = s // ch
    return pl.pallas_call(
        functools.partial(_index_body, topk=topk, ch=ch),
        grid=(b, nqb),
        in_specs=[pl.BlockSpec((1, 1, IDX_HEADS * Q_BLOCK, IDX_DIM), lambda bb, i: (bb, i, 0, 0)),
                  pl.BlockSpec((s, IDX_DIM), lambda bb, i: (bb, 0)),
                  pl.BlockSpec((1, 1, IDX_HEADS, Q_BLOCK), lambda bb, i: (bb, i, 0, 0))],
        out_specs=pl.BlockSpec((1, 1, nkc, ch, Q_BLOCK), lambda bb, i: (bb, i, 0, 0, 0)),
        out_shape=jax.ShapeDtypeStruct((b, nqb, nkc, ch, Q_BLOCK), BF16),
        scratch_shapes=[pltpu.VMEM((nkc, ch, Q_BLOCK), I32), pltpu.VMEM((8, Q_BLOCK), I32)],
        compiler_params=_params(("parallel", "arbitrary"), 48),
        name="index_select",
    )(qi, ki, wi_t)


def _attn_body(pi_ref, pk_ref, q_ref, k_ref, v_ref, b_ref, o_ref, lhs_scr, m_scr, acc_scr, *, tq, tk):
    step = pl.program_id(2)
    i = pi_ref[step]
    kc = pk_ref[step]
    last = ((i + 1) * tq - 1) // tk
    nqb = tq // Q_BLOCK
    rows = GROUP * Q_BLOCK

    @pl.when(kc == 0)
    def _():
        r = lax.broadcasted_iota(I32, (rows, Q_BLOCK), 0)
        c = lax.broadcasted_iota(I32, (rows, Q_BLOCK), 1)
        eye = jnp.where(r % Q_BLOCK == c, 1.0, 0.0).astype(BF16)
        for j in range(nqb):
            lhs_scr[j] = jnp.concatenate([q_ref[0, j], eye], axis=1)
        m_scr[...] = jnp.full(m_scr.shape, NEG, F32)
        acc_scr[...] = jnp.zeros(acc_scr.shape, F32)

    k = k_ref[...]
    v_one = jnp.concatenate([v_ref[...], jnp.ones((tk, LANES), BF16)], axis=1)
    for j in range(nqb):
        rhs = jnp.concatenate([k, b_ref[0, j, 0]], axis=1)
        s = lax.dot_general(lhs_scr[j], rhs, (((1,), (1,)), ((), ())), preferred_element_type=F32)
        m_prev = m_scr[j]
        m_new = jnp.maximum(m_prev, jnp.max(s, axis=1, keepdims=True))
        alpha = jnp.exp2(m_prev - m_new)
        p = jnp.exp2(s - jnp.concatenate([m_new] * (tk // LANES), axis=1))
        acc_scr[j] = (jnp.concatenate([alpha, alpha], axis=1) * acc_scr[j]
                      + jnp.dot(p.astype(BF16), v_one, preferred_element_type=F32))
        m_scr[j] = m_new

    @pl.when(kc == last)
    def _():
        for j in range(nqb):
            acc = acc_scr[j]
            o = acc[:, 0:HEAD_DIM] / acc[:, HEAD_DIM:2 * HEAD_DIM]
            for h in range(GROUP):
                o_ref[j * Q_BLOCK:(j + 1) * Q_BLOCK, h * HEAD_DIM:(h + 1) * HEAD_DIM] = (
                    o[h * Q_BLOCK:(h + 1) * Q_BLOCK].astype(o_ref.dtype))


def masked_attention(q, k, za, v_col0, bias, b, s, tq, tk):
    nqt = s // tq
    nkc = s // tk
    t = b * s
    pairs = [(i, kc) for i in range(nqt) for kc in range(((i + 1) * tq - 1) // tk + 1)]
    pi = jnp.asarray([p[0] for p in pairs], I32)
    pk = jnp.asarray([p[1] for p in pairs], I32)
    return pl.pallas_call(
        functools.partial(_attn_body, tq=tq, tk=tk),
        grid_spec=pltpu.PrefetchScalarGridSpec(
            num_scalar_prefetch=2,
            grid=(b, N_KV_HEADS, len(pairs)),
            in_specs=[pl.BlockSpec((1, tq // Q_BLOCK, GROUP * Q_BLOCK, HEAD_DIM),
                                   lambda bb, g, p, pi, pk: (bb, pi[p], g, 0)),
                      pl.BlockSpec((tk, HEAD_DIM), lambda bb, g, p, pi, pk: (bb * nkc + pk[p], g)),
                      pl.BlockSpec((tk, HEAD_DIM), lambda bb, g, p, pi, pk: (bb * nkc + pk[p], v_col0 + g)),
                      pl.BlockSpec((1, tq // Q_BLOCK, 1, tk, Q_BLOCK),
                                   lambda bb, g, p, pi, pk: (bb, pi[p], pk[p], 0, 0))],
            out_specs=pl.BlockSpec((tq, GROUP * HEAD_DIM), lambda bb, g, p, pi, pk: (bb * nqt + pi[p], g)),
            scratch_shapes=[pltpu.VMEM((tq // Q_BLOCK, GROUP * Q_BLOCK, 2 * HEAD_DIM), BF16),
                            pltpu.VMEM((tq // Q_BLOCK, GROUP * Q_BLOCK, LANES), F32),
                            pltpu.VMEM((tq // Q_BLOCK, GROUP * Q_BLOCK, 2 * HEAD_DIM), F32)]),
        out_shape=jax.ShapeDtypeStruct((t, N_HEADS * HEAD_DIM), BF16),
        compiler_params=_params(("parallel", "parallel", "arbitrary"), 48),
        name="masked_attention",
    )(pi, pk, q, k, za, bias)


def _merge_body(cb_ref, cc_ref, cx_ref, gc_ref, ga_ref, cch_ref, cxh_ref, ya_ref, cw_ref, wo_ref, x_ref,
                o_ref, *, tm, tiles_per_seq):
    i = pl.program_id(0)
    p = cc_ref[...].astype(F32) * cx_ref[...].astype(F32)
    halo = cch_ref[...].astype(F32) * cxh_ref[...].astype(F32)
    halo = jnp.where(i % tiles_per_seq == 0, 0.0, halo)
    row = lax.broadcasted_iota(I32, p.shape, 0)
    p1 = jnp.where(row == 0, halo[7:8], pltpu.roll(p, 1, 0))
    p2 = jnp.where(row == 0, halo[6:7], jnp.where(row == 1, halo[7:8], pltpu.roll(p, 2, 0)))
    cw = cw_ref[...]
    conv = cw[0:1] * p2 + cw[1:2] * p1 + cw[2:3] * p
    merged = (jax.nn.sigmoid(gc_ref[...].astype(F32)) * (cb_ref[...].astype(F32) * conv)
              + jax.nn.sigmoid(ga_ref[...].astype(F32)) * ya_ref[...].astype(F32))
    o_ref[...] = x_ref[...] + jnp.dot(merged.astype(BF16), wo_ref[...], preferred_element_type=F32)


def merge_outproj(za, y_attn, conv_w, w_out, x, s, tm):
    t, c = x.shape
    hb = tm // 8
    col = lambda j: (lambda i: (i, j))
    halo = lambda j: (lambda i: (jnp.maximum(i * hb - 1, 0), j))
    return pl.pallas_call(
        functools.partial(_merge_body, tm=tm, tiles_per_seq=s // tm),
        grid=(t // tm,),
        in_specs=[pl.BlockSpec((tm, c), col(0)), pl.BlockSpec((tm, c), col(1)), pl.BlockSpec((tm, c), col(2)),
                  pl.BlockSpec((tm, c), col(3)), pl.BlockSpec((tm, c), col(4)),
                  pl.BlockSpec((8, c), halo(1)), pl.BlockSpec((8, c), halo(2)),
                  pl.BlockSpec((tm, c), lambda i: (i, 0)),
                  pl.BlockSpec((CONV_WIDTH, c), lambda i: (0, 0)),
                  pl.BlockSpec((c, c), lambda i: (0, 0)),
                  pl.BlockSpec((tm, c), lambda i: (i, 0))],
        out_specs=pl.BlockSpec((tm, c), lambda i: (i, 0)),
        out_shape=jax.ShapeDtypeStruct((t, c), F32),
        compiler_params=_params(("parallel",), 56),
        name="merge_outproj",
    )(za, za, za, za, za, za, za, y_attn, conv_w, w_out, x)


def _xattn_body(h_ref, g_ref, wq_ref, kv_ref, wo_ref, o_ref):
    h = h_ref[...]
    hn = _rms(h, g_ref[...]).astype(BF16)
    q = jnp.dot(hn, wq_ref[...], preferred_element_type=F32) * (X_HEAD_DIM ** -0.5)
    q = q.astype(BF16)
    kv = kv_ref[...]
    kw = X_HEADS * X_HEAD_DIM
    outs = []
    for hh in range(X_HEADS):
        sl = slice(hh * X_HEAD_DIM, (hh + 1) * X_HEAD_DIM)
        s = lax.dot_general(q[:, sl], kv[:, sl], (((1,), (1,)), ((), ())), preferred_element_type=F32)
        m = jnp.max(s, axis=1, keepdims=True)
        p = jnp.exp(s - m)
        l = jnp.sum(p, axis=1, keepdims=True)
        vh = kv[:, kw + hh * X_HEAD_DIM: kw + (hh + 1) * X_HEAD_DIM]
        outs.append((jnp.dot(p.astype(BF16), vh, preferred_element_type=F32) / l).astype(BF16))
    o = jnp.concatenate(outs, axis=1)
    o_ref[...] = h + jnp.dot(o, wo_ref[...], preferred_element_type=F32)


def cross_attention(h, g, wq, kv, wo, s, tm):
    t, d = h.shape
    m = kv.shape[0] // (t // s)
    kw = X_HEADS * X_HEAD_DIM
    tps = s // tm
    return pl.pallas_call(
        _xattn_body,
        grid=(t // tm,),
        in_specs=[pl.BlockSpec((tm, d), lambda i: (i, 0)),
                  pl.BlockSpec((1, d), lambda i: (0, 0)),
                  pl.BlockSpec((d, kw), lambda i: (0, 0)),
                  pl.BlockSpec((m, 2 * kw), lambda i: (i // tps, 0)),
                  pl.BlockSpec((kw, d), lambda i: (0, 0))],
        out_specs=pl.BlockSpec((tm, d), lambda i: (i, 0)),
        out_shape=jax.ShapeDtypeStruct((t, d), F32),
        compiler_params=_params(("parallel",), 48),
        name="cross_attention",
    )(h, g.reshape(1, d), wq, kv, wo)


def _router_body(h_ref, g_ref, wr_ref, br_ref, hf_ref, idx_ref, gate_ref):
    hf = _rms(h_ref[...], g_ref[...])
    hf_ref[...] = hf.astype(BF16)
    logits = jnp.dot(hf, wr_ref[...], preferred_element_type=F32, precision=lax.Precision.HIGHEST) + br_ref[...]
    lane = lax.broadcasted_iota(I32, logits.shape, 1)
    lane_f = lane.astype(F32)
    l = jnp.where(lane < N_EXPERTS, logits, -jnp.inf)
    vals, idxs = [], []
    for _ in range(TOP_K_EXPERTS):
        m = jnp.max(l, axis=1, keepdims=True)
        ix = jnp.min(jnp.where(l == m, lane_f, float(LANES)), axis=1, keepdims=True)
        vals.append(m)
        idxs.append(ix)
        l = jnp.where(lane_f == ix, -jnp.inf, l)
    es = [jnp.exp(v - vals[0]) for v in vals]
    den = es[0] + es[1] + es[2] + es[3]
    idx_o = jnp.zeros(logits.shape, F32)
    gate_o = jnp.zeros(logits.shape, F32)
    for kk in range(TOP_K_EXPERTS):
        idx_o = jnp.where(lane == kk, idxs[kk], idx_o)
        gate_o = jnp.where(lane == kk, es[kk] / den, gate_o)
    idx_ref[...] = idx_o.astype(I32)
    gate_ref[...] = gate_o


def ffn_router(h, g, w_router, b_router, tm):
    t, d = h.shape
    wr = jnp.zeros((d, LANES), F32).at[:, :N_EXPERTS].set(w_router)
    br = jnp.zeros((1, LANES), F32).at[0, :N_EXPERTS].set(b_router)
    return pl.pallas_call(
        _router_body,
        grid=(t // tm,),
        in_specs=[pl.BlockSpec((tm, d), lambda i: (i, 0)),
                  pl.BlockSpec((1, d), lambda i: (0, 0)),
                  pl.BlockSpec((d, LANES), lambda i: (0, 0)),
                  pl.BlockSpec((1, LANES), lambda i: (0, 0))],
        out_specs=[pl.BlockSpec((tm, d), lambda i: (i, 0)),
                   pl.BlockSpec((tm, LANES), lambda i: (i, 0)),
                   pl.BlockSpec((tm, LANES), lambda i: (i, 0))],
        out_shape=[jax.ShapeDtypeStruct((t, d), BF16),
                   jax.ShapeDtypeStruct((t, LANES), I32),
                   jax.ShapeDtypeStruct((t, LANES), F32)],
        compiler_params=_params(("parallel",), 48),
        name="ffn_router",
    )(h, g.reshape(1, d), wr, br)


def _expert_changed(te_ref, i):
    return (i == 0) | (te_ref[i] != te_ref[jnp.maximum(i - 1, 0)])


def _gmm1_body(te_ref, nu_ref, x_ref, wg_ref, wu_ref, bg_ref, bu_ref, o_ref, wg_s, wu_s):
    i = pl.program_id(1)

    @pl.when(i < nu_ref[0])
    def _():
        @pl.when(_expert_changed(te_ref, i))
        def _():
            wg_s[...] = wg_ref[0].astype(BF16)
            wu_s[...] = wu_ref[0].astype(BF16)

        x = x_ref[...]
        gate = jnp.dot(x, wg_s[...], preferred_element_type=F32) + bg_ref[0]
        up = jnp.dot(x, wu_s[...], preferred_element_type=F32) + bu_ref[0]
        gate = jnp.minimum(gate, SWIGLU_LIMIT)
        up = jnp.clip(up, -SWIGLU_LIMIT, SWIGLU_LIMIT)
        o_ref[...] = ((up + 1.0) * (gate * jax.nn.sigmoid(SWIGLU_ALPHA * gate))).astype(o_ref.dtype)


def _gmm2_body(te_ref, nu_ref, h_ref, wd_ref, bd_ref, o_ref, wd_s):
    i = pl.program_id(1)

    @pl.when(i < nu_ref[0])
    def _():
        @pl.when(_expert_changed(te_ref, i))
        def _():
            wd_s[...] = wd_ref[0].astype(BF16)

        o_ref[...] = (jnp.dot(h_ref[...], wd_s[...], preferred_element_type=F32) + bd_ref[0]).astype(o_ref.dtype)


def expert_ffn(xs, tile_e, n_used, w_gu, b_gu, w_dn, b_dn, tf, tn):
    cap, d = xs.shape
    dff = w_dn.shape[1]
    n_tiles = cap // EXPERT_BLOCK
    nj = dff // tf
    tile = lambda i, nu: jnp.minimum(i, nu[0] - 1)
    b_gu3 = b_gu.reshape(N_EXPERTS, 1, 2 * dff)
    b_dn3 = b_dn.reshape(N_EXPERTS, 1, d)
    hdn = pl.pallas_call(
        _gmm1_body,
        grid_spec=pltpu.PrefetchScalarGridSpec(
            num_scalar_prefetch=2,
            grid=(nj, n_tiles),
            in_specs=[pl.BlockSpec((EXPERT_BLOCK, d), lambda j, i, te, nu: (tile(i, nu), 0)),
                      pl.BlockSpec((1, d, tf), lambda j, i, te, nu: (te[tile(i, nu)], 0, j)),
                      pl.BlockSpec((1, d, tf), lambda j, i, te, nu: (te[tile(i, nu)], 0, nj + j)),
                      pl.BlockSpec((1, 1, tf), lambda j, i, te, nu: (te[tile(i, nu)], 0, j)),
                      pl.BlockSpec((1, 1, tf), lambda j, i, te, nu: (te[tile(i, nu)], 0, nj + j))],
            out_specs=pl.BlockSpec((EXPERT_BLOCK, tf), lambda j, i, te, nu: (tile(i, nu), j)),
            scratch_shapes=[pltpu.VMEM((d, tf), BF16), pltpu.VMEM((d, tf), BF16)]),
        out_shape=jax.ShapeDtypeStruct((cap, dff), BF16),
        compiler_params=_params(("arbitrary", "arbitrary"), 60),
        name="expert_gate_up",
    )(tile_e, n_used, xs, w_gu, w_gu, b_gu3, b_gu3)
    nn = d // tn
    return pl.pallas_call(
        _gmm2_body,
        grid_spec=pltpu.PrefetchScalarGridSpec(
            num_scalar_prefetch=2,
            grid=(nn, n_tiles),
            in_specs=[pl.BlockSpec((EXPERT_BLOCK, dff), lambda j, i, te, nu: (tile(i, nu), 0)),
                      pl.BlockSpec((1, dff, tn), lambda j, i, te, nu: (te[tile(i, nu)], 0, j)),
                      pl.BlockSpec((1, 1, tn), lambda j, i, te, nu: (te[tile(i, nu)], 0, j))],
            out_specs=pl.BlockSpec((EXPERT_BLOCK, tn), lambda j, i, te, nu: (tile(i, nu), j)),
            scratch_shapes=[pltpu.VMEM((dff, tn), BF16)]),
        out_shape=jax.ShapeDtypeStruct((cap, d), BF16),
        compiler_params=_params(("arbitrary", "arbitrary"), 60),
        name="expert_down",
    )(tile_e, n_used, hdn, w_dn, b_dn3)


def _combine_body(h_ref, y0_ref, y1_ref, y2_ref, y3_ref, gate_ref, g_ref, o_ref, *, final):
    acc = h_ref[...]
    gate = gate_ref[...]
    for kk, y_ref in enumerate((y0_ref, y1_ref, y2_ref, y3_ref)):
        acc = acc + gate[:, kk:kk + 1] * y_ref[...].astype(F32)
    o_ref[...] = _rms(acc, g_ref[...]) if final else acc


def combine_norm(h, ys, gates, g, tm, final):
    t, d = h.shape
    return pl.pallas_call(
        functools.partial(_combine_body, final=final),
        grid=(t // tm,),
        in_specs=[pl.BlockSpec((tm, d), lambda i: (i, 0))] * (1 + TOP_K_EXPERTS)
                 + [pl.BlockSpec((tm, LANES), lambda i: (i, 0)),
                    pl.BlockSpec((1, d), lambda i: (0, 0))],
        out_specs=pl.BlockSpec((tm, d), lambda i: (i, 0)),
        out_shape=jax.ShapeDtypeStruct((t, d), F32),
        compiler_params=_params(("parallel",), 48),
        name="combine_norm",
    )(h, *ys, gates, g.reshape(1, d))


def _rope_table(positions, rot_dim, head_dim, active_lanes):
    half = rot_dim // 2
    inv_freq = jnp.float32(ROPE_THETA) ** (-(jnp.arange(half, dtype=F32) * 2.0 / rot_dim))
    ang = positions.astype(F32).reshape(-1)[:, None] * inv_freq
    cos, sin = jnp.cos(ang), jnp.sin(ang)
    t = cos.shape[0]
    one = jnp.ones((t, head_dim - rot_dim), F32)
    zero = jnp.zeros((t, head_dim - rot_dim), F32)
    zh = jnp.zeros((t, half), F32)
    c = jnp.concatenate([cos, cos, one], axis=1)
    s1 = jnp.concatenate([zh, sin, zero], axis=1)
    s2 = jnp.concatenate([-sin, zh, zero], axis=1)
    reps = active_lanes // head_dim

    def widen(a, fill):
        a = jnp.tile(a, (1, reps))
        return jnp.concatenate([a, jnp.full((t, LANES - active_lanes), fill, F32)], axis=1)

    return jnp.concatenate([widen(c, 1.0), widen(s1, 0.0), widen(s2, 0.0)], axis=1)


def _route(idx, t, n_tiles):
    flat_e = idx.reshape(-1)
    onehot = (flat_e[:, None] == jnp.arange(N_EXPERTS, dtype=I32)[None, :]).astype(I32)
    csum = jnp.cumsum(onehot, axis=0)
    rank = jnp.take_along_axis(csum, flat_e[:, None], axis=1)[:, 0] - 1
    counts = csum[-1]
    padded = (counts + EXPERT_BLOCK - 1) // EXPERT_BLOCK * EXPERT_BLOCK
    pend = jnp.cumsum(padded)
    pstart = pend - padded
    dest = pstart[flat_e] + rank
    cap = n_tiles * EXPERT_BLOCK
    buf_tok = (jnp.arange(cap, dtype=I32) % t).at[dest].set(jnp.arange(flat_e.shape[0], dtype=I32) // TOP_K_EXPERTS)
    tile_start = jnp.arange(n_tiles, dtype=I32) * EXPERT_BLOCK
    tile_e = jnp.minimum(jnp.sum((pend[None, :] <= tile_start[:, None]).astype(I32), axis=1), N_EXPERTS - 1)
    n_used = (pend[-1] // EXPERT_BLOCK).astype(I32).reshape(1)
    return dest, buf_tok, tile_e, n_used


def kernel(x, mem, positions, norm_mix, w_in, conv_w, w_out, norm_xattn, norm_mem, wq_x, wk_x, wv_x, wo_x,
           norm_ffn, w_router, b_router, w_gate_up, b_gate_up, w_down, b_down, norm_final):
    b, s, d = x.shape
    t = b * s
    assert d == N_HEADS * HEAD_DIM and s % 512 == 0
    x2 = x.reshape(t, d)
    h = x2
    for l in range(w_in.shape[0]):
        wl = w_in[l]
        o = [0]
        for n in (d, d, d, d, N_KV_HEADS * HEAD_DIM, N_KV_HEADS * HEAD_DIM, IDX_HEADS * IDX_DIM, IDX_DIM, IDX_HEADS, d, d):
            o.append(o[-1] + n)
        seg = lambda a: wl[:, o[a]:o[a + 1]]
        w_a = jnp.concatenate([seg(0), seg(1), seg(2), seg(9), seg(10), seg(5)], axis=1).astype(BF16)
        pad_b = jnp.zeros((d, LANES - IDX_DIM - IDX_HEADS + 128), F32)
        w_b = jnp.concatenate([seg(3), seg(6), seg(4), seg(7), seg(8), pad_b], axis=1).astype(BF16)
        za = norm_matmul(h, norm_mix[l], w_a, BF16, 1024, 1536)
        zb = norm_matmul(h, norm_mix[l], w_b, F32, 1024, 1280)
        tq = _rope_table(positions, ROT_DIM, HEAD_DIM, LANES)
        ti = _rope_table(positions, IDX_ROT_DIM, IDX_DIM, LANES)
        tk = _rope_table(positions, IDX_ROT_DIM, IDX_DIM, IDX_DIM)
        tq_rows = 512
        q_r, qi_r, k_r, ki_r, wi_r = rope_split(zb, tq, ti, tk, b, s, tq_rows)
        topk = min(TOPK_MAX, s // 4)
        tkc = 512
        wi_t = wi_r.reshape(b, s // Q_BLOCK, Q_BLOCK, IDX_HEADS).transpose(0, 1, 3, 2)
        bias = index_select(qi_r, ki_r, wi_t, b, s, topk, tkc)
        y_attn = masked_attention(q_r, k_r, za, 5 * d // HEAD_DIM, bias, b, s, tq_rows, tkc)
        h = merge_outproj(za, y_attn, conv_w[l], w_out[l].astype(BF16), h, s, 256)
        kv_w = jnp.concatenate([wk_x[l], wv_x[l]], axis=1).astype(BF16)
        kv = norm_matmul(mem.reshape(-1, d), norm_mem[l], kv_w, BF16, 1024, 512)
        h = cross_attention(h, norm_xattn[l], wq_x[l].astype(BF16), kv, wo_x[l].astype(BF16), s, 256)
        hf, idx, gates = ffn_router(h, norm_ffn[l], w_router[l], b_router[l], 256)
        n_tiles = -(-(t * TOP_K_EXPERTS) // EXPERT_BLOCK) + N_EXPERTS
        dest, buf_tok, tile_e, n_used = _route(idx[:, :TOP_K_EXPERTS], t, n_tiles)
        xs = hf[buf_tok]
        y = expert_ffn(xs, tile_e, n_used, w_gate_up[l], b_gate_up[l], w_down[l], b_down[l], 1024, 2048)
        dest2 = dest.reshape(t, TOP_K_EXPERTS)
        ys = [y[dest2[:, kk]] for kk in range(TOP_K_EXPERTS)]
        h = combine_norm(h, ys, gates, norm_final, 256, final=(l + 1 == w_in.shape[0]))
    return h.reshape(b, s, d)
```

```python
import functools

import jax
import jax.numpy as jnp
from jax import lax
from jax.experimental import pallas as pl
from jax.experimental.pallas import tpu as pltpu

F32 = jnp.float32
BF16 = jnp.bfloat16
I32 = jnp.int32

HEAD_DIM = 128
N_HEADS = 16
N_KV_HEADS = 4
GROUP = N_HEADS // N_KV_HEADS
ROT_DIM = 32
IDX_HEADS = 16
IDX_DIM = 64
IDX_ROT_DIM = 16
TOPK_MAX = 256
Q_BLOCK = 128
ROPE_THETA = 500000.0
X_HEADS = 4
X_HEAD_DIM = 128
N_EXPERTS = 32
TOP_K_EXPERTS = 4
SWIGLU_LIMIT = 7.0
SWIGLU_ALPHA = 1.702
EXPERT_BLOCK = 512
NORM_EPS = 1e-6
CONV_WIDTH = 3

LANES = 128
NEG = -1e30
INT_MIN = -2 ** 31
MIB = 1024 * 1024
LOG2E = 1.4426950408889634


def _params(sem, vmem_mib):
    return pltpu.CompilerParams(dimension_semantics=sem, vmem_limit_bytes=vmem_mib * MIB)


def _rms(x, g):
    ms = jnp.mean(x * x, axis=-1, keepdims=True)
    return x * lax.rsqrt(ms + NORM_EPS) * g


def _norm_matmul_body(x_ref, g_ref, w_ref, o_ref, a_scr):
    @pl.when(pl.program_id(1) == 0)
    def _():
        a_scr[...] = _rms(x_ref[...], g_ref[...]).astype(BF16)

    o_ref[...] = jnp.dot(a_scr[...], w_ref[...], preferred_element_type=F32).astype(o_ref.dtype)


def norm_matmul(x, g, w, out_dtype, tm, tn):
    m, k = x.shape
    n = w.shape[1]
    tm = min(tm, m)
    assert m % tm == 0 and n % tn == 0
    return pl.pallas_call(
        _norm_matmul_body,
        grid=(m // tm, n // tn),
        in_specs=[pl.BlockSpec((tm, k), lambda i, j: (i, 0)),
                  pl.BlockSpec((1, k), lambda i, j: (0, 0)),
                  pl.BlockSpec((k, tn), lambda i, j: (0, j))],
        out_specs=pl.BlockSpec((tm, tn), lambda i, j: (i, j)),
        out_shape=jax.ShapeDtypeStruct((m, n), out_dtype),
        scratch_shapes=[pltpu.VMEM((tm, k), BF16)],
        compiler_params=_params(("parallel", "arbitrary"), 48),
        name="norm_matmul",
    )(x, g.reshape(1, k), w)


def _rope(x, tab, shift):
    c, s1, s2 = tab[:, 0:LANES], tab[:, LANES:2 * LANES], tab[:, 2 * LANES:3 * LANES]
    return x * c + pltpu.roll(x, shift, 1) * s1 + pltpu.roll(x, LANES - shift, 1) * s2


def _rope_body(q_ref, qi_ref, k_ref, kw_ref, tq_ref, ti_ref, tk_ref,
               qo_ref, qio_ref, ko_ref, kio_ref, wio_ref, *, tm):
    tq = tq_ref[...]
    scale = HEAD_DIM ** -0.5 * LOG2E
    for h in range(N_HEADS):
        xh = q_ref[:, h * LANES:(h + 1) * LANES]
        xr = (_rope(xh, tq, ROT_DIM // 2) * scale).astype(BF16)
        for qb in range(tm // Q_BLOCK):
            qo_ref[0, qb, h * Q_BLOCK:(h + 1) * Q_BLOCK, :] = xr[qb * Q_BLOCK:(qb + 1) * Q_BLOCK]
    for g in range(N_KV_HEADS):
        xh = k_ref[:, g * LANES:(g + 1) * LANES]
        ko_ref[:, g * LANES:(g + 1) * LANES] = _rope(xh, tq, ROT_DIM // 2).astype(BF16)
    ti = ti_ref[...]
    for j in range(IDX_HEADS * IDX_DIM // LANES):
        xr = _rope(qi_ref[:, j * LANES:(j + 1) * LANES], ti, IDX_ROT_DIM // 2).astype(BF16)
        for half in range(LANES // IDX_DIM):
            h = j * (LANES // IDX_DIM) + half
            part = xr[:, half * IDX_DIM:(half + 1) * IDX_DIM]
            for qb in range(tm // Q_BLOCK):
                qio_ref[0, qb, h * Q_BLOCK:(h + 1) * Q_BLOCK, :] = part[qb * Q_BLOCK:(qb + 1) * Q_BLOCK]
    kw = kw_ref[...]
    kr = _rope(kw, tk_ref[...], IDX_ROT_DIM // 2)
    kio_ref[...] = kr[:, 0:IDX_DIM].astype(BF16)
    wio_ref[...] = kw[:, IDX_DIM:IDX_DIM + IDX_HEADS] * (IDX_HEADS ** -0.5 * IDX_DIM ** -0.5)


def rope_split(zb, tq, ti, tk, b, s, tm):
    t = b * s
    nq = s // tm
    d = N_HEADS * HEAD_DIM
    qi_w = IDX_HEADS * IDX_DIM
    kv_w = N_KV_HEADS * HEAD_DIM
    off_qi = d // qi_w
    off_k = (d + qi_w) // kv_w
    off_kw = (d + qi_w + kv_w) // LANES
    row = lambda bb, i: bb * nq + i
    return pl.pallas_call(
        functools.partial(_rope_body, tm=tm),
        grid=(b, nq),
        in_specs=[pl.BlockSpec((tm, d), lambda bb, i: (row(bb, i), 0)),
                  pl.BlockSpec((tm, qi_w), lambda bb, i: (row(bb, i), off_qi)),
                  pl.BlockSpec((tm, kv_w), lambda bb, i: (row(bb, i), off_k)),
                  pl.BlockSpec((tm, LANES), lambda bb, i: (row(bb, i), off_kw)),
                  pl.BlockSpec((tm, 3 * LANES), lambda bb, i: (row(bb, i), 0)),
                  pl.BlockSpec((tm, 3 * LANES), lambda bb, i: (row(bb, i), 0)),
                  pl.BlockSpec((tm, 3 * LANES), lambda bb, i: (row(bb, i), 0))],
        out_specs=[pl.BlockSpec((1, tm // Q_BLOCK, N_HEADS * Q_BLOCK, HEAD_DIM), lambda bb, i: (bb, i, 0, 0)),
                   pl.BlockSpec((1, tm // Q_BLOCK, IDX_HEADS * Q_BLOCK, IDX_DIM), lambda bb, i: (bb, i, 0, 0)),
                   pl.BlockSpec((tm, kv_w), lambda bb, i: (row(bb, i), 0)),
                   pl.BlockSpec((tm, IDX_DIM), lambda bb, i: (row(bb, i), 0)),
                   pl.BlockSpec((tm, IDX_HEADS), lambda bb, i: (row(bb, i), 0))],
        out_shape=[jax.ShapeDtypeStruct((b, s // Q_BLOCK, N_HEADS * Q_BLOCK, HEAD_DIM), BF16),
                   jax.ShapeDtypeStruct((b, s // Q_BLOCK, IDX_HEADS * Q_BLOCK, IDX_DIM), BF16),
                   jax.ShapeDtypeStruct((t, kv_w), BF16),
                   jax.ShapeDtypeStruct((t, IDX_DIM), BF16),
                   jax.ShapeDtypeStruct((t, IDX_HEADS), F32)],
        compiler_params=_params(("parallel", "parallel"), 48),
        name="rope_split",
    )(zb, zb, zb, zb, tq, ti, tk)


def _index_body(qi_ref, ki_ref, wi_ref, o_ref, skey_ref, tie_ref, *, topk, ch):
    i = pl.program_id(1)
    nkc = skey_ref.shape[0]
    pos_bits = (nkc * ch).bit_length()
    nch = (i * Q_BLOCK + Q_BLOCK + ch - 1) // ch
    qi = qi_ref[0, 0]
    wi = wi_ref[0, 0]
    key_pos = lax.broadcasted_iota(I32, (ch, Q_BLOCK), 0)
    q_pos = i * Q_BLOCK + lax.broadcasted_iota(I32, (ch, Q_BLOCK), 1)

    def score_chunk(c, carry):
        kc = ki_ref[pl.ds(pl.multiple_of(c * ch, ch), ch), :]
        logits = lax.dot_general(kc, qi, (((1,), (1,)), ((), ())), preferred_element_type=F32)
        acc = jnp.zeros((ch, Q_BLOCK), F32)
        for h in range(IDX_HEADS):
            acc = acc + jnp.maximum(logits[:, h * Q_BLOCK:(h + 1) * Q_BLOCK], 0.0) * wi[h:h + 1, :]
        bits = lax.bitcast_convert_type(acc, I32)
        key = bits ^ ((bits >> 31) & 0x7FFFFFFF)
        skey_ref[c] = jnp.where(c * ch + key_pos <= q_pos, key, INT_MIN)
        return carry

    lax.fori_loop(0, nch, score_chunk, 0)

    def count(pred):
        def body(c, cnt):
            m = jnp.where(pred(c, skey_ref[c]), 1.0, 0.0)
            return cnt + jnp.sum(m.reshape(ch // 64, 64, Q_BLOCK), axis=0)
        cnt = lax.fori_loop(0, nch, body, jnp.zeros((64, Q_BLOCK), F32))
        return jnp.sum(cnt, axis=0, keepdims=True)

    def bit_body(t, u):
        cand_u = u | jnp.left_shift(jnp.int32(1), 31 - t)
        cand = cand_u ^ INT_MIN
        cnt = count(lambda c, keys: keys >= cand)
        return jnp.where(cnt >= topk, cand_u, u)

    u = lax.fori_loop(0, 32, bit_body, jnp.zeros((1, Q_BLOCK), I32))
    thr = jnp.maximum(u ^ INT_MIN, INT_MIN + 1)

    n_ge = count(lambda c, keys: keys >= thr)
    tie_ref[...] = jnp.full(tie_ref.shape, nkc * ch, I32)

    @pl.when(jnp.max(n_ge) > topk)
    def _():
        need = topk - count(lambda c, keys: keys > thr)

        def pos_body(t, end):
            cand = end + jnp.left_shift(jnp.int32(1), pos_bits - 1 - t)
            cnt = count(lambda c, keys: (keys == thr) & (c * ch + key_pos < cand))
            return jnp.where(cnt <= need, cand, end)

        end = lax.fori_loop(0, pos_bits, pos_body, jnp.zeros((1, Q_BLOCK), I32))
        tie_ref[...] = jnp.broadcast_to(jnp.where(n_ge > topk, end, nkc * ch), tie_ref.shape)

    tie_end = tie_ref[0:1, :]
    o_ref[...] = jnp.full(o_ref.shape, NEG, o_ref.dtype)

    def write_chunk(c, carry):
        keys = skey_ref[c]
        keep = (keys > thr) | ((keys == thr) & (c * ch + key_pos < tie_end))
        o_ref[0, 0, c] = jnp.where(keep, 0.0, NEG).astype(o_ref.dtype)
        return carry

    lax.fori_loop(0, nch, write_chunk, 0)


def index_select(qi, ki, wi_t, b, s, topk, ch):
    nqb = s // Q_BLOCK
    nkc = s // ch
    return pl.pallas_call(
        functools.partial(_index_body, topk=topk, ch=ch),
        grid=(b, nqb),
        in_specs=[pl.BlockSpec((1, 1, IDX_HEADS * Q_BLOCK, IDX_DIM), lambda bb, i: (bb, i, 0, 0)),
                  pl.BlockSpec((s, IDX_DIM), lambda bb, i: (bb, 0)),
                  pl.BlockSpec((1, 1, IDX_HEADS, Q_BLOCK), lambda bb, i: (bb, i, 0, 0))],
        out_specs=pl.BlockSpec((1, 1, nkc, ch, Q_BLOCK), lambda bb, i: (bb, i, 0, 0, 0)),
        out_shape=jax.ShapeDtypeStruct((b, nqb, nkc, ch, Q_BLOCK), BF16),
        scratch_shapes=[pltpu.VMEM((nkc, ch, Q_BLOCK), I32), pltpu.VMEM((8, Q_BLOCK), I32)],
        compiler_params=_params(("parallel", "arbitrary"), 48),
        name="index_select",
    )(qi, ki, wi_t)


def _attn_body(pi_ref, pk_ref, q_ref, k_ref, v_ref, b_ref, o_ref, lhs_scr, m_scr, acc_scr, *, tq, tk):
    step = pl.program_id(2)
    i = pi_ref[step]
    kc = pk_ref[step]
    last = ((i + 1) * tq - 1) // tk
    nqb = tq // Q_BLOCK
    rows = GROUP * Q_BLOCK

    @pl.when(kc == 0)
    def _():
        r = lax.broadcasted_iota(I32, (rows, Q_BLOCK), 0)
        c = lax.broadcasted_iota(I32, (rows, Q_BLOCK), 1)
        eye = jnp.where(r % Q_BLOCK == c, 1.0, 0.0).astype(BF16)
        for j in range(nqb):
            lhs_scr[j] = jnp.concatenate([q_ref[0, j], eye], axis=1)
        m_scr[...] = jnp.full(m_scr.shape, NEG, F32)
        acc_scr[...] = jnp.zeros(acc_scr.shape, F32)

    k = k_ref[...]
    v_one = jnp.concatenate([v_ref[...], jnp.ones((tk, LANES), BF16)], axis=1)
    for j in range(nqb):
        rhs = jnp.concatenate([k, b_ref[0, j, 0]], axis=1)
        s = lax.dot_general(lhs_scr[j], rhs, (((1,), (1,)), ((), ())), preferred_element_type=F32)
        m_prev = m_scr[j]
        m_new = jnp.maximum(m_prev, jnp.max(s, axis=1, keepdims=True))
        alpha = jnp.exp2(m_prev - m_new)
        p = jnp.exp2(s - jnp.concatenate([m_new] * (tk // LANES), axis=1))
        acc_scr[j] = (jnp.concatenate([alpha, alpha], axis=1) * acc_scr[j]
                      + jnp.dot(p.astype(BF16), v_one, preferred_element_type=F32))
        m_scr[j] = m_new

    @pl.when(kc == last)
    def _():
        for j in range(nqb):
            acc = acc_scr[j]
            o = acc[:, 0:HEAD_DIM] / acc[:, HEAD_DIM:2 * HEAD_DIM]
            for h in range(GROUP):
                o_ref[j * Q_BLOCK:(j + 1) * Q_BLOCK, h * HEAD_DIM:(h + 1) * HEAD_DIM] = (
                    o[h * Q_BLOCK:(h + 1) * Q_BLOCK].astype(o_ref.dtype))


def masked_attention(q, k, za, v_col0, bias, b, s, tq, tk):
    nqt = s // tq
    nkc = s // tk
    t = b * s
    pairs = [(i, kc) for i in range(nqt) for kc in range(((i + 1) * tq - 1) // tk + 1)]
    pi = jnp.asarray([p[0] for p in pairs], I32)
    pk = jnp.asarray([p[1] for p in pairs], I32)
    return pl.pallas_call(
        functools.partial(_attn_body, tq=tq, tk=tk),
        grid_spec=pltpu.PrefetchScalarGridSpec(
            num_scalar_prefetch=2,
            grid=(b, N_KV_HEADS, len(pairs)),
            in_specs=[pl.BlockSpec((1, tq // Q_BLOCK, GROUP * Q_BLOCK, HEAD_DIM),
                                   lambda bb, g, p, pi, pk: (bb, pi[p], g, 0)),
                      pl.BlockSpec((tk, HEAD_DIM), lambda bb, g, p, pi, pk: (bb * nkc + pk[p], g)),
                      pl.BlockSpec((tk, HEAD_DIM), lambda bb, g, p, pi, pk: (bb * nkc + pk[p], v_col0 + g)),
                      pl.BlockSpec((1, tq // Q_BLOCK, 1, tk, Q_BLOCK),
                                   lambda bb, g, p, pi, pk: (bb, pi[p], pk[p], 0, 0))],
            out_specs=pl.BlockSpec((tq, GROUP * HEAD_DIM), lambda bb, g, p, pi, pk: (bb * nqt + pi[p], g)),
            scratch_shapes=[pltpu.VMEM((tq // Q_BLOCK, GROUP * Q_BLOCK, 2 * HEAD_DIM), BF16),
                            pltpu.VMEM((tq // Q_BLOCK, GROUP * Q_BLOCK, LANES), F32),
                            pltpu.VMEM((tq // Q_BLOCK, GROUP * Q_BLOCK, 2 * HEAD_DIM), F32)]),
        out_shape=jax.ShapeDtypeStruct((t, N_HEADS * HEAD_DIM), BF16),
        compiler_params=_params(("parallel", "parallel", "arbitrary"), 48),
        name="masked_attention",
    )(pi, pk, q, k, za, bias)


def _merge_body(cb_ref, cc_ref, cx_ref, gc_ref, ga_ref, cch_ref, cxh_ref, ya_ref, cw_ref, wo_ref, x_ref,
                o_ref, *, tm, tiles_per_seq):
    i = pl.program_id(0)
    p = cc_ref[...].astype(F32) * cx_ref[...].astype(F32)
    halo = cch_ref[...].astype(F32) * cxh_ref[...].astype(F32)
    halo = jnp.where(i % tiles_per_seq == 0, 0.0, halo)
    row = lax.broadcasted_iota(I32, p.shape, 0)
    p1 = jnp.where(row == 0, halo[7:8], pltpu.roll(p, 1, 0))
    p2 = jnp.where(row == 0, halo[6:7], jnp.where(row == 1, halo[7:8], pltpu.roll(p, 2, 0)))
    cw = cw_ref[...]
    conv = cw[0:1] * p2 + cw[1:2] * p1 + cw[2:3] * p
    merged = (jax.nn.sigmoid(gc_ref[...].astype(F32)) * (cb_ref[...].astype(F32) * conv)
              + jax.nn.sigmoid(ga_ref[...].astype(F32)) * ya_ref[...].astype(F32))
    o_ref[...] = x_ref[...] + jnp.dot(merged.astype(BF16), wo_ref[...], preferred_element_type=F32)


def merge_outproj(za, y_attn, conv_w, w_out, x, s, tm):
    t, c = x.shape
    hb = tm // 8
    col = lambda j: (lambda i: (i, j))
    halo = lambda j: (lambda i: (jnp.maximum(i * hb - 1, 0), j))
    return pl.pallas_call(
        functools.partial(_merge_body, tm=tm, tiles_per_seq=s // tm),
        grid=(t // tm,),
        in_specs=[pl.BlockSpec((tm, c), col(0)), pl.BlockSpec((tm, c), col(1)), pl.BlockSpec((tm, c), col(2)),
                  pl.BlockSpec((tm, c), col(3)), pl.BlockSpec((tm, c), col(4)),
                  pl.BlockSpec((8, c), halo(1)), pl.BlockSpec((8, c), halo(2)),
                  pl.BlockSpec((tm, c), lambda i: (i, 0)),
                  pl.BlockSpec((CONV_WIDTH, c), lambda i: (0, 0)),
                  pl.BlockSpec((c, c), lambda i: (0, 0)),
                  pl.BlockSpec((tm, c), lambda i: (i, 0))],
        out_specs=pl.BlockSpec((tm, c), lambda i: (i, 0)),
        out_shape=jax.ShapeDtypeStruct((t, c), F32),
        compiler_params=_params(("parallel",), 56),
        name="merge_outproj",
    )(za, za, za, za, za, za, za, y_attn, conv_w, w_out, x)


def _xattn_body(h_ref, g_ref, wq_ref, kv_ref, wo_ref, o_ref):
    h = h_ref[...]
    hn = _rms(h, g_ref[...]).astype(BF16)
    q = jnp.dot(hn, wq_ref[...], preferred_element_type=F32) * (X_HEAD_DIM ** -0.5)
    q = q.astype(BF16)
    kv = kv_ref[...]
    kw = X_HEADS * X_HEAD_DIM
    outs = []
    for hh in range(X_HEADS):
        sl = slice(hh * X_HEAD_DIM, (hh + 1) * X_HEAD_DIM)
        s = lax.dot_general(q[:, sl], kv[:, sl], (((1,), (1,)), ((), ())), preferred_element_type=F32)
        m = jnp.max(s, axis=1, keepdims=True)
        p = jnp.exp(s - m)
        l = jnp.sum(p, axis=1, keepdims=True)
        vh = kv[:, kw + hh * X_HEAD_DIM: kw + (hh + 1) * X_HEAD_DIM]
        outs.append((jnp.dot(p.astype(BF16), vh, preferred_element_type=F32) / l).astype(BF16))
    o = jnp.concatenate(outs, axis=1)
    o_ref[...] = h + jnp.dot(o, wo_ref[...], preferred_element_type=F32)


def cross_attention(h, g, wq, kv, wo, s, tm):
    t, d = h.shape
    m = kv.shape[0] // (t // s)
    kw = X_HEADS * X_HEAD_DIM
    tps = s // tm
    return pl.pallas_call(
        _xattn_body,
        grid=(t // tm,),
        in_specs=[pl.BlockSpec((tm, d), lambda i: (i, 0)),
                  pl.BlockSpec((1, d), lambda i: (0, 0)),
                  pl.BlockSpec((d, kw), lambda i: (0, 0)),
                  pl.BlockSpec((m, 2 * kw), lambda i: (i // tps, 0)),
                  pl.BlockSpec((kw, d), lambda i: (0, 0))],
        out_specs=pl.BlockSpec((tm, d), lambda i: (i, 0)),
        out_shape=jax.ShapeDtypeStruct((t, d), F32),
        compiler_params=_params(("parallel",), 48),
        name="cross_attention",
    )(h, g.reshape(1, d), wq, kv, wo)


def _router_body(h_ref, g_ref, wr_ref, br_ref, hf_ref, idx_ref, gate_ref):
    hf = _rms(h_ref[...], g_ref[...])
    hf_ref[...] = hf.astype(BF16)
    logits = jnp.dot(hf, wr_ref[...], preferred_element_type=F32, precision=lax.Precision.HIGHEST) + br_ref[...]
    lane = lax.broadcasted_iota(I32, logits.shape, 1)
    lane_f = lane.astype(F32)
    l = jnp.where(lane < N_EXPERTS, logits, -jnp.inf)
    vals, idxs = [], []
    for _ in range(TOP_K_EXPERTS):
        m = jnp.max(l, axis=1, keepdims=True)
        ix = jnp.min(jnp.where(l == m, lane_f, float(LANES)), axis=1, keepdims=True)
        vals.append(m)
        idxs.append(ix)
        l = jnp.where(lane_f == ix, -jnp.inf, l)
    es = [jnp.exp(v - vals[0]) for v in vals]
    den = es[0] + es[1] + es[2] + es[3]
    idx_o = jnp.zeros(logits.shape, F32)
    gate_o = jnp.zeros(logits.shape, F32)
    for kk in range(TOP_K_EXPERTS):
        idx_o = jnp.where(lane == kk, idxs[kk], idx_o)
        gate_o = jnp.where(lane == kk, es[kk] / den, gate_o)
    idx_ref[...] = idx_o.astype(I32)
    gate_ref[...] = gate_o


def ffn_router(h, g, w_router, b_router, tm):
    t, d = h.shape
    wr = jnp.zeros((d, LANES), F32).at[:, :N_EXPERTS].set(w_router)
    br = jnp.zeros((1, LANES), F32).at[0, :N_EXPERTS].set(b_router)
    return pl.pallas_call(
        _router_body,
        grid=(t // tm,),
        in_specs=[pl.BlockSpec((tm, d), lambda i: (i, 0)),
                  pl.BlockSpec((1, d), lambda i: (0, 0)),
                  pl.BlockSpec((d, LANES), lambda i: (0, 0)),
                  pl.BlockSpec((1, LANES), lambda i: (0, 0))],
        out_specs=[pl.BlockSpec((tm, d), lambda i: (i, 0)),
                   pl.BlockSpec((tm, LANES), lambda i: (i, 0)),
                   pl.BlockSpec((tm, LANES), lambda i: (i, 0))],
        out_shape=[jax.ShapeDtypeStruct((t, d), BF16),
                   jax.ShapeDtypeStruct((t, LANES), I32),
                   jax.ShapeDtypeStruct((t, LANES), F32)],
        compiler_params=_params(("parallel",), 48),
        name="ffn_router",
    )(h, g.reshape(1, d), wr, br)


def _expert_changed(te_ref, i):
    return (i == 0) | (te_ref[i] != te_ref[jnp.maximum(i - 1, 0)])


def _gmm1_body(te_ref, nu_ref, x_ref, wg_ref, wu_ref, bg_ref, bu_ref, o_ref, wg_s, wu_s):
    i = pl.program_id(1)

    @pl.when(i < nu_ref[0])
    def _():
        @pl.when(_expert_changed(te_ref, i))
        def _():
            wg_s[...] = wg_ref[0].astype(BF16)
            wu_s[...] = wu_ref[0].astype(BF16)

        x = x_ref[...]
        gate = jnp.dot(x, wg_s[...], preferred_element_type=F32) + bg_ref[0]
        up = jnp.dot(x, wu_s[...], preferred_element_type=F32) + bu_ref[0]
        gate = jnp.minimum(gate, SWIGLU_LIMIT)
        up = jnp.clip(up, -SWIGLU_LIMIT, SWIGLU_LIMIT)
        o_ref[...] = ((up + 1.0) * (gate * jax.nn.sigmoid(SWIGLU_ALPHA * gate))).astype(o_ref.dtype)


def _gmm2_body(te_ref, nu_ref, h_ref, wd_ref, bd_ref, o_ref, wd_s):
    i = pl.program_id(1)

    @pl.when(i < nu_ref[0])
    def _():
        @pl.when(_expert_changed(te_ref, i))
        def _():
            wd_s[...] = wd_ref[0].astype(BF16)

        o_ref[...] = (jnp.dot(h_ref[...], wd_s[...], preferred_element_type=F32) + bd_ref[0]).astype(o_ref.dtype)


def expert_ffn(xs, tile_e, n_used, w_gu, b_gu, w_dn, b_dn, tf, tn):
    cap, d = xs.shape
    dff = w_dn.shape[1]
    n_tiles = cap // EXPERT_BLOCK
    nj = dff // tf
    tile = lambda i, nu: jnp.minimum(i, nu[0] - 1)
    b_gu3 = b_gu.reshape(N_EXPERTS, 1, 2 * dff)
    b_dn3 = b_dn.reshape(N_EXPERTS, 1, d)
    hdn = pl.pallas_call(
        _gmm1_body,
        grid_spec=pltpu.PrefetchScalarGridSpec(
            num_scalar_prefetch=2,
            grid=(nj, n_tiles),
            in_specs=[pl.BlockSpec((EXPERT_BLOCK, d), lambda j, i, te, nu: (tile(i, nu), 0)),
                      pl.BlockSpec((1, d, tf), lambda j, i, te, nu: (te[tile(i, nu)], 0, j)),
                      pl.BlockSpec((1, d, tf), lambda j, i, te, nu: (te[tile(i, nu)], 0, nj + j)),
                      pl.BlockSpec((1, 1, tf), lambda j, i, te, nu: (te[tile(i, nu)], 0, j)),
                      pl.BlockSpec((1, 1, tf), lambda j, i, te, nu: (te[tile(i, nu)], 0, nj + j))],
            out_specs=pl.BlockSpec((EXPERT_BLOCK, tf), lambda j, i, te, nu: (tile(i, nu), j)),
            scratch_shapes=[pltpu.VMEM((d, tf), BF16), pltpu.VMEM((d, tf), BF16)]),
        out_shape=jax.ShapeDtypeStruct((cap, dff), BF16),
        compiler_params=_params(("arbitrary", "arbitrary"), 60),
        name="expert_gate_up",
    )(tile_e, n_used, xs, w_gu, w_gu, b_gu3, b_gu3)
    nn = d // tn
    return pl.pallas_call(
        _gmm2_body,
        grid_spec=pltpu.PrefetchScalarGridSpec(
            num_scalar_prefetch=2,
            grid=(nn, n_tiles),
            in_specs=[pl.BlockSpec((EXPERT_BLOCK, dff), lambda j, i, te, nu: (tile(i, nu), 0)),
                      pl.BlockSpec((1, dff, tn), lambda j, i, te, nu: (te[tile(i, nu)], 0, j)),
                      pl.BlockSpec((1, 1, tn), lambda j, i, te, nu: (te[tile(i, nu)], 0, j))],
            out_specs=pl.BlockSpec((EXPERT_BLOCK, tn), lambda j, i, te, nu: (tile(i, nu), j)),
            scratch_shapes=[pltpu.VMEM((dff, tn), BF16)]),
        out_shape=jax.ShapeDtypeStruct((cap, d), BF16),
        compiler_params=_params(("arbitrary", "arbitrary"), 60),
        name="expert_down",
    )(tile_e, n_used, hdn, w_dn, b_dn3)


def _combine_body(h_ref, y0_ref, y1_ref, y2_ref, y3_ref, gate_ref, g_ref, o_ref, *, final):
    acc = h_ref[...]
    gate = gate_ref[...]
    for kk, y_ref in enumerate((y0_ref, y1_ref, y2_ref, y3_ref)):
        acc = acc + gate[:, kk:kk + 1] * y_ref[...].astype(F32)
    o_ref[...] = _rms(acc, g_ref[...]) if final else acc


def combine_norm(h, ys, gates, g, tm, final):
    t, d = h.shape
    return pl.pallas_call(
        functools.partial(_combine_body, final=final),
        grid=(t // tm,),
        in_specs=[pl.BlockSpec((tm, d), lambda i: (i, 0))] * (1 + TOP_K_EXPERTS)
                 + [pl.BlockSpec((tm, LANES), lambda i: (i, 0)),
                    pl.BlockSpec((1, d), lambda i: (0, 0))],
        out_specs=pl.BlockSpec((tm, d), lambda i: (i, 0)),
        out_shape=jax.ShapeDtypeStruct((t, d), F32),
        compiler_params=_params(("parallel",), 48),
        name="combine_norm",
    )(h, *ys, gates, g.reshape(1, d))


def _rope_table(positions, rot_dim, head_dim, active_lanes):
    half = rot_dim // 2
    inv_freq = jnp.float32(ROPE_THETA) ** (-(jnp.arange(half, dtype=F32) * 2.0 / rot_dim))
    ang = positions.astype(F32).reshape(-1)[:, None] * inv_freq
    cos, sin = jnp.cos(ang), jnp.sin(ang)
    t = cos.shape[0]
    one = jnp.ones((t, head_dim - rot_dim), F32)
    zero = jnp.zeros((t, head_dim - rot_dim), F32)
    zh = jnp.zeros((t, half), F32)
    c = jnp.concatenate([cos, cos, one], axis=1)
    s1 = jnp.concatenate([zh, sin, zero], axis=1)
    s2 = jnp.concatenate([-sin, zh, zero], axis=1)
    reps = active_lanes // head_dim

    def widen(a, fill):
        a = jnp.tile(a, (1, reps))
        return jnp.concatenate([a, jnp.full((t, LANES - active_lanes), fill, F32)], axis=1)

    return jnp.concatenate([widen(c, 1.0), widen(s1, 0.0), widen(s2, 0.0)], axis=1)


def _route(idx, t, n_tiles):
    flat_e = idx.reshape(-1)
    onehot = (flat_e[:, None] == jnp.arange(N_EXPERTS, dtype=I32)[None, :]).astype(I32)
    csum = jnp.cumsum(onehot, axis=0)
    rank = jnp.take_along_axis(csum, flat_e[:, None], axis=1)[:, 0] - 1
    counts = csum[-1]
    padded = (counts + EXPERT_BLOCK - 1) // EXPERT_BLOCK * EXPERT_BLOCK
    pend = jnp.cumsum(padded)
    pstart = pend - padded
    dest = pstart[flat_e] + rank
    cap = n_tiles * EXPERT_BLOCK
    buf_tok = (jnp.arange(cap, dtype=I32) % t).at[dest].set(jnp.arange(flat_e.shape[0], dtype=I32) // TOP_K_EXPERTS)
    tile_start = jnp.arange(n_tiles, dtype=I32) * EXPERT_BLOCK
    tile_e = jnp.minimum(jnp.sum((pend[None, :] <= tile_start[:, None]).astype(I32), axis=1), N_EXPERTS - 1)
    n_used = (pend[-1] // EXPERT_BLOCK).astype(I32).reshape(1)
    return dest, buf_tok, tile_e, n_used


def kernel(x, mem, positions, norm_mix, w_in, conv_w, w_out, norm_xattn, norm_mem, wq_x, wk_x, wv_x, wo_x,
           norm_ffn, w_router, b_router, w_gate_up, b_gate_up, w_down, b_down, norm_final):
    b, s, d = x.shape
    t = b * s
    assert d == N_HEADS * HEAD_DIM and s % 512 == 0
    x2 = x.reshape(t, d)
    h = x2
    for l in range(w_in.shape[0]):
        wl = w_in[l]
        o = [0]
        for n in (d, d, d, d, N_KV_HEADS * HEAD_DIM, N_KV_HEADS * HEAD_DIM, IDX_HEADS * IDX_DIM, IDX_DIM, IDX_HEADS, d, d):
            o.append(o[-1] + n)
        seg = lambda a: wl[:, o[a]:o[a + 1]]
        w_a = jnp.concatenate([seg(0), seg(1), seg(2), seg(9), seg(10), seg(5)], axis=1).astype(BF16)
        pad_b = jnp.zeros((d, LANES - IDX_DIM - IDX_HEADS + 128), F32)
        w_b = jnp.concatenate([seg(3), seg(6), seg(4), seg(7), seg(8), pad_b], axis=1).astype(BF16)
        za = norm_matmul(h, norm_mix[l], w_a, BF16, 1024, 1536)
        zb = norm_matmul(h, norm_mix[l], w_b, F32, 1024, 1280)
        tq = _rope_table(positions, ROT_DIM, HEAD_DIM, LANES)
        ti = _rope_table(positions, IDX_ROT_DIM, IDX_DIM, LANES)
        tk = _rope_table(positions, IDX_ROT_DIM, IDX_DIM, IDX_DIM)
        tq_rows = 512
        q_r, qi_r, k_r, ki_r, wi_r = rope_split(zb, tq, ti, tk, b, s, tq_rows)
        topk = min(TOPK_MAX, s // 4)
        tkc = 512
        wi_t = wi_r.reshape(b, s // Q_BLOCK, Q_BLOCK, IDX_HEADS).transpose(0, 1, 3, 2)
        bias = index_select(qi_r, ki_r, wi_t, b, s, topk, tkc)
        y_attn = masked_attention(q_r, k_r, za, 5 * d // HEAD_DIM, bias, b, s, tq_rows, tkc)
        h = merge_outproj(za, y_attn, conv_w[l], w_out[l].astype(BF16), h, s, 256)
        kv_w = jnp.concatenate([wk_x[l], wv_x[l]], axis=1).astype(BF16)
        kv = norm_matmul(mem.reshape(-1, d), norm_mem[l], kv_w, BF16, 1024, 512)
        h = cross_attention(h, norm_xattn[l], wq_x[l].astype(BF16), kv, wo_x[l].astype(BF16), s, 256)
        hf, idx, gates = ffn_router(h, norm_ffn[l], w_router[l], b_router[l], 256)
        n_tiles = -(-(t * TOP_K_EXPERTS) // EXPERT_BLOCK) + N_EXPERTS
        dest, buf_tok, tile_e, n_used = _route(idx[:, :TOP_K_EXPERTS], t, n_tiles)
        xs = hf[buf_tok]
        y = expert_ffn(xs, tile_e, n_used, w_gate_up[l], b_gate_up[l], w_down[l], b_down[l], 1024, 2048)
        dest2 = dest.reshape(t, TOP_K_EXPERTS)
        ys = [y[dest2[:, kk]] for kk in range(TOP_K_EXPERTS)]
        h = combine_norm(h, ys, gates, norm_final, 256, final=(l + 1 == w_in.shape[0]))
    return h.reshape(b, s, d)
```

```python
import functools

import jax
import jax.numpy as jnp
from jax import lax
from jax.experimental import pallas as pl
from jax.experimental.pallas import tpu as pltpu

F32 = jnp.float32
BF16 = jnp.bfloat16
I32 = jnp.int32

HEAD_DIM = 128
N_HEADS = 16
N_KV_HEADS = 4
GROUP = N_HEADS // N_KV_HEADS
ROT_DIM = 32
IDX_HEADS = 16
IDX_DIM = 64
IDX_ROT_DIM = 16
TOPK_MAX = 256
Q_BLOCK = 128
ROPE_THETA = 500000.0
X_HEADS = 4
X_HEAD_DIM = 128
N_EXPERTS = 32
TOP_K_EXPERTS = 4
SWIGLU_LIMIT = 7.0
SWIGLU_ALPHA = 1.702
EXPERT_BLOCK = 512
NORM_EPS = 1e-6
CONV_WIDTH = 3

LANES = 128
NEG = -1e30
INT_MIN = -2 ** 31
MIB = 1024 * 1024
LOG2E = 1.4426950408889634


def _params(sem, vmem_mib):
    return pltpu.CompilerParams(dimension_semantics=sem, vmem_limit_bytes=vmem_mib * MIB)


def _rms(x, g):
    ms = jnp.mean(x * x, axis=-1, keepdims=True)
    return x * lax.rsqrt(ms + NORM_EPS) * g


def _norm_matmul_body(x_ref, g_ref, w_ref, o_ref, a_scr):
    @pl.when(pl.program_id(1) == 0)
    def _():
        a_scr[...] = _rms(x_ref[...], g_ref[...]).astype(BF16)

    o_ref[...] = jnp.dot(a_scr[...], w_ref[...], preferred_element_type=F32).astype(o_ref.dtype)


def norm_matmul(x, g, w, out_dtype, tm, tn):
    m, k = x.shape
    n = w.shape[1]
    tm = min(tm, m)
    assert m % tm == 0 and n % tn == 0
    return pl.pallas_call(
        _norm_matmul_body,
        grid=(m // tm, n // tn),
        in_specs=[pl.BlockSpec((tm, k), lambda i, j: (i, 0)),
                  pl.BlockSpec((1, k), lambda i, j: (0, 0)),
                  pl.BlockSpec((k, tn), lambda i, j: (0, j))],
        out_specs=pl.BlockSpec((tm, tn), lambda i, j: (i, j)),
        out_shape=jax.ShapeDtypeStruct((m, n), out_dtype),
        scratch_shapes=[pltpu.VMEM((tm, k), BF16)],
        compiler_params=_params(("parallel", "arbitrary"), 48),
        name="norm_matmul",
    )(x, g.reshape(1, k), w)


def _rope(x, tab, shift):
    c, s1, s2 = tab[:, 0:LANES], tab[:, LANES:2 * LANES], tab[:, 2 * LANES:3 * LANES]
    return x * c + pltpu.roll(x, shift, 1) * s1 + pltpu.roll(x, LANES - shift, 1) * s2


def _rope_body(q_ref, qi_ref, k_ref, kw_ref, tq_ref, ti_ref, tk_ref,
               qo_ref, qio_ref, ko_ref, kio_ref, wio_ref, *, tm):
    tq = tq_ref[...]
    scale = HEAD_DIM ** -0.5 * LOG2E
    for h in range(N_HEADS):
        xh = q_ref[:, h * LANES:(h + 1) * LANES]
        xr = (_rope(xh, tq, ROT_DIM // 2) * scale).astype(BF16)
        for qb in range(tm // Q_BLOCK):
            qo_ref[0, qb, h * Q_BLOCK:(h + 1) * Q_BLOCK, :] = xr[qb * Q_BLOCK:(qb + 1) * Q_BLOCK]
    for g in range(N_KV_HEADS):
        xh = k_ref[:, g * LANES:(g + 1) * LANES]
        ko_ref[:, g * LANES:(g + 1) * LANES] = _rope(xh, tq, ROT_DIM // 2).astype(BF16)
    ti = ti_ref[...]
    for j in range(IDX_HEADS * IDX_DIM // LANES):
        xr = _rope(qi_ref[:, j * LANES:(j + 1) * LANES], ti, IDX_ROT_DIM // 2).astype(BF16)
        for half in range(LANES // IDX_DIM):
            h = j * (LANES // IDX_DIM) + half
            part = xr[:, half * IDX_DIM:(half + 1) * IDX_DIM]
            for qb in range(tm // Q_BLOCK):
                qio_ref[0, qb, h * Q_BLOCK:(h + 1) * Q_BLOCK, :] = part[qb * Q_BLOCK:(qb + 1) * Q_BLOCK]
    kw = kw_ref[...]
    kr = _rope(kw, tk_ref[...], IDX_ROT_DIM // 2)
    kio_ref[...] = kr[:, 0:IDX_DIM].astype(BF16)
    wio_ref[...] = kw[:, IDX_DIM:IDX_DIM + IDX_HEADS] * (IDX_HEADS ** -0.5 * IDX_DIM ** -0.5)


def rope_split(zb, tq, ti, tk, b, s, tm):
    t = b * s
    nq = s // tm
    d = N_HEADS * HEAD_DIM
    qi_w = IDX_HEADS * IDX_DIM
    kv_w = N_KV_HEADS * HEAD_DIM
    off_qi = d // qi_w
    off_k = (d + qi_w) // kv_w
    off_kw = (d + qi_w + kv_w) // LANES
    row = lambda bb, i: bb * nq + i
    return pl.pallas_call(
        functools.partial(_rope_body, tm=tm),
        grid=(b, nq),
        in_specs=[pl.BlockSpec((tm, d), lambda bb, i: (row(bb, i), 0)),
                  pl.BlockSpec((tm, qi_w), lambda bb, i: (row(bb, i), off_qi)),
                  pl.BlockSpec((tm, kv_w), lambda bb, i: (row(bb, i), off_k)),
                  pl.BlockSpec((tm, LANES), lambda bb, i: (row(bb, i), off_kw)),
                  pl.BlockSpec((tm, 3 * LANES), lambda bb, i: (row(bb, i), 0)),
                  pl.BlockSpec((tm, 3 * LANES), lambda bb, i: (row(bb, i), 0)),
                  pl.BlockSpec((tm, 3 * LANES), lambda bb, i: (row(bb, i), 0))],
        out_specs=[pl.BlockSpec((1, tm // Q_BLOCK, N_HEADS * Q_BLOCK, HEAD_DIM), lambda bb, i: (bb, i, 0, 0)),
                   pl.BlockSpec((1, tm // Q_BLOCK, IDX_HEADS * Q_BLOCK, IDX_DIM), lambda bb, i: (bb, i, 0, 0)),
                   pl.BlockSpec((tm, kv_w), lambda bb, i: (row(bb, i), 0)),
                   pl.BlockSpec((tm, IDX_DIM), lambda bb, i: (row(bb, i), 0)),
                   pl.BlockSpec((tm, IDX_HEADS), lambda bb, i: (row(bb, i), 0))],
        out_shape=[jax.ShapeDtypeStruct((b, s // Q_BLOCK, N_HEADS * Q_BLOCK, HEAD_DIM), BF16),
                   jax.ShapeDtypeStruct((b, s // Q_BLOCK, IDX_HEADS * Q_BLOCK, IDX_DIM), BF16),
                   jax.ShapeDtypeStruct((t, kv_w), BF16),
                   jax.ShapeDtypeStruct((t, IDX_DIM), BF16),
                   jax.ShapeDtypeStruct((t, IDX_HEADS), F32)],
        compiler_params=_params(("parallel", "parallel"), 48),
        name="rope_split",
    )(zb, zb, zb, zb, tq, ti, tk)


def _index_body(qi_ref, ki_ref, wi_ref, o_ref, skey_ref, tie_ref, *, topk, ch):
    i = pl.program_id(1)
    nkc = skey_ref.shape[0]
    pos_bits = (nkc * ch).bit_length()
    nch = (i * Q_BLOCK + Q_BLOCK + ch - 1) // ch
    qi = qi_ref[0, 0]
    wi = wi_ref[0, 0]
    key_pos = lax.broadcasted_iota(I32, (ch, Q_BLOCK), 0)
    q_pos = i * Q_BLOCK + lax.broadcasted_iota(I32, (ch, Q_BLOCK), 1)

    def score_chunk(c, carry):
        kc = ki_ref[pl.ds(pl.multiple_of(c * ch, ch), ch), :]
        logits = lax.dot_general(kc, qi, (((1,), (1,)), ((), ())), preferred_element_type=F32)
        acc = jnp.zeros((ch, Q_BLOCK), F32)
        for h in range(IDX_HEADS):
            acc = acc + jnp.maximum(logits[:, h * Q_BLOCK:(h + 1) * Q_BLOCK], 0.0) * wi[h:h + 1, :]
        bits = lax.bitcast_convert_type(acc, I32)
        key = bits ^ ((bits >> 31) & 0x7FFFFFFF)
        skey_ref[c] = jnp.where(c * ch + key_pos <= q_pos, key, INT_MIN)
        return carry

    lax.fori_loop(0, nch, score_chunk, 0)

    def count(pred):
        def body(c, cnt):
            m = jnp.where(pred(c, skey_ref[c]), 1.0, 0.0)
            return cnt + jnp.sum(m.reshape(ch // 64, 64, Q_BLOCK), axis=0)
        cnt = lax.fori_loop(0, nch, body, jnp.zeros((64, Q_BLOCK), F32))
        return jnp.sum(cnt, axis=0, keepdims=True)

    def bit_body(t, u):
        cand_u = u | jnp.left_shift(jnp.int32(1), 31 - t)
        cand = cand_u ^ INT_MIN
        cnt = count(lambda c, keys: keys >= cand)
        return jnp.where(cnt >= topk, cand_u, u)

    u = lax.fori_loop(0, 32, bit_body, jnp.zeros((1, Q_BLOCK), I32))
    thr = jnp.maximum(u ^ INT_MIN, INT_MIN + 1)

    n_ge = count(lambda c, keys: keys >= thr)
    tie_ref[...] = jnp.full(tie_ref.shape, nkc * ch, I32)

    @pl.when(jnp.max(n_ge) > topk)
    def _():
        need = topk - count(lambda c, keys: keys > thr)

        def pos_body(t, end):
            cand = end + jnp.left_shift(jnp.int32(1), pos_bits - 1 - t)
            cnt = count(lambda c, keys: (keys == thr) & (c * ch + key_pos < cand))
            return jnp.where(cnt <= need, cand, end)

        end = lax.fori_loop(0, pos_bits, pos_body, jnp.zeros((1, Q_BLOCK), I32))
        tie_ref[...] = jnp.broadcast_to(jnp.where(n_ge > topk, end, nkc * ch), tie_ref.shape)

    tie_end = tie_ref[0:1, :]
    o_ref[...] = jnp.full(o_ref.shape, NEG, o_ref.dtype)

    def write_chunk(c, carry):
        keys = skey_ref[c]
        keep = (keys > thr) | ((keys == thr) & (c * ch + key_pos < tie_end))
        o_ref[0, 0, c] = jnp.where(keep, 0.0, NEG).astype(o_ref.dtype)
        return carry

    lax.fori_loop(0, nch, write_chunk, 0)


def index_select(qi, ki, wi_t, b, s, topk, ch):
    nqb = s // Q_BLOCK
    nkc = s // ch
    return pl.pallas_call(
        functools.partial(_index_body, topk=topk, ch=ch),
        grid=(b, nqb),
        in_specs=[pl.BlockSpec((1, 1, IDX_HEADS * Q_BLOCK, IDX_DIM), lambda bb, i: (bb, i, 0, 0)),
                  pl.BlockSpec((s, IDX_DIM), lambda bb, i: (bb, 0)),
                  pl.BlockSpec((1, 1, IDX_HEADS, Q_BLOCK), lambda bb, i: (bb, i, 0, 0))],
        out_specs=pl.BlockSpec((1, 1, nkc, ch, Q_BLOCK), lambda bb, i: (bb, i, 0, 0, 0)),
        out_shape=jax.ShapeDtypeStruct((b, nqb, nkc, ch, Q_BLOCK), BF16),
        scratch_shapes=[pltpu.VMEM((nkc, ch, Q_BLOCK), I32), pltpu.VMEM((8, Q_BLOCK), I32)],
        compiler_params=_params(("parallel", "arbitrary"), 48),
        name="index_select",
    )(qi, ki, wi_t)


def _attn_body(pi_ref, pk_ref, q_ref, k_ref, v_ref, b_ref, o_ref, lhs_scr, m_scr, acc_scr, *, tq, tk):
    step = pl.program_id(2)
    i = pi_ref[step]
    kc = pk_ref[step]
    last = ((i + 1) * tq - 1) // tk
    nqb = tq // Q_BLOCK
    rows = GROUP * Q_BLOCK

    @pl.when(kc == 0)
    def _():
        r = lax.broadcasted_iota(I32, (rows, Q_BLOCK), 0)
        c = lax.broadcasted_iota(I32, (rows, Q_BLOCK), 1)
        eye = jnp.where(r % Q_BLOCK == c, 1.0, 0.0).astype(BF16)
        for j in range(nqb):
            lhs_scr[j] = jnp.concatenate([q_ref[0, j], eye], axis=1)
        m_scr[...] = jnp.full(m_scr.shape, NEG, F32)
        acc_scr[...] = jnp.zeros(acc_scr.shape, F32)

    k = k_ref[...]
    v_one = jnp.concatenate([v_ref[...], jnp.ones((tk, LANES), BF16)], axis=1)
    for j in range(nqb):
        rhs = jnp.concatenate([k, b_ref[0, j, 0]], axis=1)
        s = lax.dot_general(lhs_scr[j], rhs, (((1,), (1,)), ((), ())), preferred_element_type=F32)
        m_prev = m_scr[j]
        m_new = jnp.maximum(m_prev, jnp.max(s, axis=1, keepdims=True))
        alpha = jnp.exp2(m_prev - m_new)
        p = jnp.exp2(s - jnp.concatenate([m_new] * (tk // LANES), axis=1))
        acc_scr[j] = (jnp.concatenate([alpha, alpha], axis=1) * acc_scr[j]
                      + jnp.dot(p.astype(BF16), v_one, preferred_element_type=F32))
        m_scr[j] = m_new

    @pl.when(kc == last)
    def _():
        for j in range(nqb):
            acc = acc_scr[j]
            o = acc[:, 0:HEAD_DIM] / acc[:, HEAD_DIM:2 * HEAD_DIM]
            for h in range(GROUP):
                o_ref[j * Q_BLOCK:(j + 1) * Q_BLOCK, h * HEAD_DIM:(h + 1) * HEAD_DIM] = (
                    o[h * Q_BLOCK:(h + 1) * Q_BLOCK].astype(o_ref.dtype))


def masked_attention(q, k, za, v_col0, bias, b, s, tq, tk):
    nqt = s // tq
    nkc = s // tk
    t = b * s
    pairs = [(i, kc) for i in range(nqt) for kc in range(((i + 1) * tq - 1) // tk + 1)]
    pi = jnp.asarray([p[0] for p in pairs], I32)
    pk = jnp.asarray([p[1] for p in pairs], I32)
    return pl.pallas_call(
        functools.partial(_attn_body, tq=tq, tk=tk),
        grid_spec=pltpu.PrefetchScalarGridSpec(
            num_scalar_prefetch=2,
            grid=(b, N_KV_HEADS, len(pairs)),
            in_specs=[pl.BlockSpec((1, tq // Q_BLOCK, GROUP * Q_BLOCK, HEAD_DIM),
                                   lambda bb, g, p, pi, pk: (bb, pi[p], g, 0)),
                      pl.BlockSpec((tk, HEAD_DIM), lambda bb, g, p, pi, pk: (bb * nkc + pk[p], g)),
                      pl.BlockSpec((tk, HEAD_DIM), lambda bb, g, p, pi, pk: (bb * nkc + pk[p], v_col0 + g)),
                      pl.BlockSpec((1, tq // Q_BLOCK, 1, tk, Q_BLOCK),
                                   lambda bb, g, p, pi, pk: (bb, pi[p], pk[p], 0, 0))],
            out_specs=pl.BlockSpec((tq, GROUP * HEAD_DIM), lambda bb, g, p, pi, pk: (bb * nqt + pi[p], g)),
            scratch_shapes=[pltpu.VMEM((tq // Q_BLOCK, GROUP * Q_BLOCK, 2 * HEAD_DIM), BF16),
                            pltpu.VMEM((tq // Q_BLOCK, GROUP * Q_BLOCK, LANES), F32),
                            pltpu.VMEM((tq // Q_BLOCK, GROUP * Q_BLOCK, 2 * HEAD_DIM), F32)]),
        out_shape=jax.ShapeDtypeStruct((t, N_HEADS * HEAD_DIM), BF16),
        compiler_params=_params(("parallel", "parallel", "arbitrary"), 48),
        name="masked_attention",
    )(pi, pk, q, k, za, bias)


def _merge_body(cb_ref, cc_ref, cx_ref, gc_ref, ga_ref, cch_ref, cxh_ref, ya_ref, cw_ref, wo_ref, x_ref,
                o_ref, *, tm, tiles_per_seq):
    i = pl.program_id(0)
    p = cc_ref[...].astype(F32) * cx_ref[...].astype(F32)
    halo = cch_ref[...].astype(F32) * cxh_ref[...].astype(F32)
    halo = jnp.where(i % tiles_per_seq == 0, 0.0, halo)
    row = lax.broadcasted_iota(I32, p.shape, 0)
    p1 = jnp.where(row == 0, halo[7:8], pltpu.roll(p, 1, 0))
    p2 = jnp.where(row == 0, halo[6:7], jnp.where(row == 1, halo[7:8], pltpu.roll(p, 2, 0)))
    cw = cw_ref[...]
    conv = cw[0:1] * p2 + cw[1:2] * p1 + cw[2:3] * p
    merged = (jax.nn.sigmoid(gc_ref[...].astype(F32)) * (cb_ref[...].astype(F32) * conv)
              + jax.nn.sigmoid(ga_ref[...].astype(F32)) * ya_ref[...].astype(F32))
    o_ref[...] = x_ref[...] + jnp.dot(merged.astype(BF16), wo_ref[...], preferred_element_type=F32)


def merge_outproj(za, y_attn, conv_w, w_out, x, s, tm):
    t, c = x.shape
    hb = tm // 8
    col = lambda j: (lambda i: (i, j))
    halo = lambda j: (lambda i: (jnp.maximum(i * hb - 1, 0), j))
    return pl.pallas_call(
        functools.partial(_merge_body, tm=tm, tiles_per_seq=s // tm),
        grid=(t // tm,),
        in_specs=[pl.BlockSpec((tm, c), col(0)), pl.BlockSpec((tm, c), col(1)), pl.BlockSpec((tm, c), col(2)),
                  pl.BlockSpec((tm, c), col(3)), pl.BlockSpec((tm, c), col(4)),
                  pl.BlockSpec((8, c), halo(1)), pl.BlockSpec((8, c), halo(2)),
                  pl.BlockSpec((tm, c), lambda i: (i, 0)),
                  pl.BlockSpec((CONV_WIDTH, c), lambda i: (0, 0)),
                  pl.BlockSpec((c, c), lambda i: (0, 0)),
                  pl.BlockSpec((tm, c), lambda i: (i, 0))],
        out_specs=pl.BlockSpec((tm, c), lambda i: (i, 0)),
        out_shape=jax.ShapeDtypeStruct((t, c), F32),
        compiler_params=_params(("parallel",), 56),
        name="merge_outproj",
    )(za, za, za, za, za, za, za, y_attn, conv_w, w_out, x)


def _xattn_body(h_ref, g_ref, wq_ref, kv_ref, wo_ref, o_ref):
    h = h_ref[...]
    hn = _rms(h, g_ref[...]).astype(BF16)
    q = jnp.dot(hn, wq_ref[...], preferred_element_type=F32) * (X_HEAD_DIM ** -0.5)
    q = q.astype(BF16)
    kv = kv_ref[...]
    kw = X_HEADS * X_HEAD_DIM
    outs = []
    for hh in range(X_HEADS):
        sl = slice(hh * X_HEAD_DIM, (hh + 1) * X_HEAD_DIM)
        s = lax.dot_general(q[:, sl], kv[:, sl], (((1,), (1,)), ((), ())), preferred_element_type=F32)
        m = jnp.max(s, axis=1, keepdims=True)
        p = jnp.exp(s - m)
        l = jnp.sum(p, axis=1, keepdims=True)
        vh = kv[:, kw + hh * X_HEAD_DIM: kw + (hh + 1) * X_HEAD_DIM]
        outs.append((jnp.dot(p.astype(BF16), vh, preferred_element_type=F32) / l).astype(BF16))
    o = jnp.concatenate(outs, axis=1)
    o_ref[...] = h + jnp.dot(o, wo_ref[...], preferred_element_type=F32)


def cross_attention(h, g, wq, kv, wo, s, tm):
    t, d = h.shape
    m = kv.shape[0] // (t // s)
    kw = X_HEADS * X_HEAD_DIM
    tps = s // tm
    return pl.pallas_call(
        _xattn_body,
        grid=(t // tm,),
        in_specs=[pl.BlockSpec((tm, d), lambda i: (i, 0)),
                  pl.BlockSpec((1, d), lambda i: (0, 0)),
                  pl.BlockSpec((d, kw), lambda i: (0, 0)),
                  pl.BlockSpec((m, 2 * kw), lambda i: (i // tps, 0)),
                  pl.BlockSpec((kw, d), lambda i: (0, 0))],
        out_specs=pl.BlockSpec((tm, d), lambda i: (i, 0)),
        out_shape=jax.ShapeDtypeStruct((t, d), F32),
        compiler_params=_params(("parallel",), 48),
        name="cross_attention",
    )(h, g.reshape(1, d), wq, kv, wo)


def _xattn_router_body(h_ref, g_ref, wq_ref, kv_ref, wo_ref, gf_ref, wr_ref, br_ref,
                       o_ref, hf_ref, idx_ref, gate_ref):
    _xattn_body(h_ref, g_ref, wq_ref, kv_ref, wo_ref, o_ref)
    _router_body(o_ref, gf_ref, wr_ref, br_ref, hf_ref, idx_ref, gate_ref)


def cross_attention_router(h, g, wq, kv, wo, gf, w_router, b_router, s, tm):
    t, d = h.shape
    m = kv.shape[0] // (t // s)
    kw = X_HEADS * X_HEAD_DIM
    tps = s // tm
    wr = jnp.zeros((d, LANES), F32).at[:, :N_EXPERTS].set(w_router)
    br = jnp.zeros((1, LANES), F32).at[0, :N_EXPERTS].set(b_router)
    return pl.pallas_call(
        _xattn_router_body,
        grid=(t // tm,),
        in_specs=[pl.BlockSpec((tm, d), lambda i: (i, 0)),
                  pl.BlockSpec((1, d), lambda i: (0, 0)),
                  pl.BlockSpec((d, kw), lambda i: (0, 0)),
                  pl.BlockSpec((m, 2 * kw), lambda i: (i // tps, 0)),
                  pl.BlockSpec((kw, d), lambda i: (0, 0)),
                  pl.BlockSpec((1, d), lambda i: (0, 0)),
                  pl.BlockSpec((d, LANES), lambda i: (0, 0)),
                  pl.BlockSpec((1, LANES), lambda i: (0, 0))],
        out_specs=[pl.BlockSpec((tm, d), lambda i: (i, 0)),
                   pl.BlockSpec((tm, d), lambda i: (i, 0)),
                   pl.BlockSpec((tm, LANES), lambda i: (i, 0)),
                   pl.BlockSpec((tm, LANES), lambda i: (i, 0))],
        out_shape=[jax.ShapeDtypeStruct((t, d), F32),
                   jax.ShapeDtypeStruct((t, d), BF16),
                   jax.ShapeDtypeStruct((t, LANES), I32),
                   jax.ShapeDtypeStruct((t, LANES), F32)],
        compiler_params=_params(("parallel",), 48),
        name="cross_attention_router",
    )(h, g.reshape(1, d), wq, kv, wo, gf.reshape(1, d), wr, br)


def _router_body(h_ref, g_ref, wr_ref, br_ref, hf_ref, idx_ref, gate_ref):
    hf = _rms(h_ref[...], g_ref[...])
    hf_ref[...] = hf.astype(BF16)
    logits = jnp.dot(hf, wr_ref[...], preferred_element_type=F32, precision=lax.Precision.HIGHEST) + br_ref[...]
    lane = lax.broadcasted_iota(I32, logits.shape, 1)
    lane_f = lane.astype(F32)
    l = jnp.where(lane < N_EXPERTS, logits, -jnp.inf)
    vals, idxs = [], []
    for _ in range(TOP_K_EXPERTS):
        m = jnp.max(l, axis=1, keepdims=True)
        ix = jnp.min(jnp.where(l == m, lane_f, float(LANES)), axis=1, keepdims=True)
        vals.append(m)
        idxs.append(ix)
        l = jnp.where(lane_f == ix, -jnp.inf, l)
    es = [jnp.exp(v - vals[0]) for v in vals]
    den = es[0] + es[1] + es[2] + es[3]
    idx_o = jnp.zeros(logits.shape, F32)
    gate_o = jnp.zeros(logits.shape, F32)
    for kk in range(TOP_K_EXPERTS):
        idx_o = jnp.where(lane == kk, idxs[kk], idx_o)
        gate_o = jnp.where(lane == kk, es[kk] / den, gate_o)
    idx_ref[...] = idx_o.astype(I32)
    gate_ref[...] = gate_o


def ffn_router(h, g, w_router, b_router, tm):
    t, d = h.shape
    wr = jnp.zeros((d, LANES), F32).at[:, :N_EXPERTS].set(w_router)
    br = jnp.zeros((1, LANES), F32).at[0, :N_EXPERTS].set(b_router)
    return pl.pallas_call(
        _router_body,
        grid=(t // tm,),
        in_specs=[pl.BlockSpec((tm, d), lambda i: (i, 0)),
                  pl.BlockSpec((1, d), lambda i: (0, 0)),
                  pl.BlockSpec((d, LANES), lambda i: (0, 0)),
                  pl.BlockSpec((1, LANES), lambda i: (0, 0))],
        out_specs=[pl.BlockSpec((tm, d), lambda i: (i, 0)),
                   pl.BlockSpec((tm, LANES), lambda i: (i, 0)),
                   pl.BlockSpec((tm, LANES), lambda i: (i, 0))],
        out_shape=[jax.ShapeDtypeStruct((t, d), BF16),
                   jax.ShapeDtypeStruct((t, LANES), I32),
                   jax.ShapeDtypeStruct((t, LANES), F32)],
        compiler_params=_params(("parallel",), 48),
        name="ffn_router",
    )(h, g.reshape(1, d), wr, br)


def _expert_changed(te_ref, i):
    return (i == 0) | (te_ref[i] != te_ref[jnp.maximum(i - 1, 0)])


def _gmm1_body(te_ref, nu_ref, x_ref, wg_ref, wu_ref, bg_ref, bu_ref, o_ref, wg_s, wu_s):
    i = pl.program_id(1)

    @pl.when(i < nu_ref[0])
    def _():
        @pl.when(_expert_changed(te_ref, i))
        def _():
            wg_s[...] = wg_ref[0].astype(BF16)
            wu_s[...] = wu_ref[0].astype(BF16)

        x = x_ref[...]
        gate = jnp.dot(x, wg_s[...], preferred_element_type=F32) + bg_ref[0]
        up = jnp.dot(x, wu_s[...], preferred_element_type=F32) + bu_ref[0]
        gate = jnp.minimum(gate, SWIGLU_LIMIT)
        up = jnp.clip(up, -SWIGLU_LIMIT, SWIGLU_LIMIT)
        o_ref[...] = ((up + 1.0) * (gate * jax.nn.sigmoid(SWIGLU_ALPHA * gate))).astype(o_ref.dtype)


def _gmm2_body(te_ref, nu_ref, h_ref, wd_ref, bd_ref, o_ref, wd_s):
    i = pl.program_id(1)

    @pl.when(i < nu_ref[0])
    def _():
        @pl.when(_expert_changed(te_ref, i))
        def _():
            wd_s[...] = wd_ref[0].astype(BF16)

        o_ref[...] = (jnp.dot(h_ref[...], wd_s[...], preferred_element_type=F32) + bd_ref[0]).astype(o_ref.dtype)


def expert_ffn(xs, tile_e, n_used, w_gu, b_gu, w_dn, b_dn, tf, tn):
    cap, d = xs.shape
    dff = w_dn.shape[1]
    n_tiles = cap // EXPERT_BLOCK
    nj = dff // tf
    tile = lambda i, nu: jnp.minimum(i, nu[0] - 1)
    b_gu3 = b_gu.reshape(N_EXPERTS, 1, 2 * dff)
    b_dn3 = b_dn.reshape(N_EXPERTS, 1, d)
    hdn = pl.pallas_call(
        _gmm1_body,
        grid_spec=pltpu.PrefetchScalarGridSpec(
            num_scalar_prefetch=2,
            grid=(nj, n_tiles),
            in_specs=[pl.BlockSpec((EXPERT_BLOCK, d), lambda j, i, te, nu: (tile(i, nu), 0)),
                      pl.BlockSpec((1, d, tf), lambda j, i, te, nu: (te[tile(i, nu)], 0, j)),
                      pl.BlockSpec((1, d, tf), lambda j, i, te, nu: (te[tile(i, nu)], 0, nj + j)),
                      pl.BlockSpec((1, 1, tf), lambda j, i, te, nu: (te[tile(i, nu)], 0, j)),
                      pl.BlockSpec((1, 1, tf), lambda j, i, te, nu: (te[tile(i, nu)], 0, nj + j))],
            out_specs=pl.BlockSpec((EXPERT_BLOCK, tf), lambda j, i, te, nu: (tile(i, nu), j)),
            scratch_shapes=[pltpu.VMEM((d, tf), BF16), pltpu.VMEM((d, tf), BF16)]),
        out_shape=jax.ShapeDtypeStruct((cap, dff), BF16),
        compiler_params=_params(("arbitrary", "arbitrary"), 60),
        name="expert_gate_up",
    )(tile_e, n_used, xs, w_gu, w_gu, b_gu3, b_gu3)
    nn = d // tn
    return pl.pallas_call(
        _gmm2_body,
        grid_spec=pltpu.PrefetchScalarGridSpec(
            num_scalar_prefetch=2,
            grid=(nn, n_tiles),
            in_specs=[pl.BlockSpec((EXPERT_BLOCK, dff), lambda j, i, te, nu: (tile(i, nu), 0)),
                      pl.BlockSpec((1, dff, tn), lambda j, i, te, nu: (te[tile(i, nu)], 0, j)),
                      pl.BlockSpec((1, 1, tn), lambda j, i, te, nu: (te[tile(i, nu)], 0, j))],
            out_specs=pl.BlockSpec((EXPERT_BLOCK, tn), lambda j, i, te, nu: (tile(i, nu), j)),
            scratch_shapes=[pltpu.VMEM((dff, tn), BF16)]),
        out_shape=jax.ShapeDtypeStruct((cap, d), BF16),
        compiler_params=_params(("arbitrary", "arbitrary"), 60),
        name="expert_down",
    )(tile_e, n_used, hdn, w_dn, b_dn3)


def _combine_body(h_ref, y0_ref, y1_ref, y2_ref, y3_ref, gate_ref, g_ref, o_ref, *, final):
    acc = h_ref[...]
    gate = gate_ref[...]
    for kk, y_ref in enumerate((y0_ref, y1_ref, y2_ref, y3_ref)):
        acc = acc + gate[:, kk:kk + 1] * y_ref[...].astype(F32)
    o_ref[...] = _rms(acc, g_ref[...]) if final else acc


def combine_norm(h, ys, gates, g, tm, final):
    t, d = h.shape
    return pl.pallas_call(
        functools.partial(_combine_body, final=final),
        grid=(t // tm,),
        in_specs=[pl.BlockSpec((tm, d), lambda i: (i, 0))] * (1 + TOP_K_EXPERTS)
                 + [pl.BlockSpec((tm, LANES), lambda i: (i, 0)),
                    pl.BlockSpec((1, d), lambda i: (0, 0))],
        out_specs=pl.BlockSpec((tm, d), lambda i: (i, 0)),
        out_shape=jax.ShapeDtypeStruct((t, d), F32),
        compiler_params=_params(("parallel",), 48),
        name="combine_norm",
    )(h, *ys, gates, g.reshape(1, d))


def _rope_table(positions, rot_dim, head_dim, active_lanes):
    half = rot_dim // 2
    inv_freq = jnp.float32(ROPE_THETA) ** (-(jnp.arange(half, dtype=F32) * 2.0 / rot_dim))
    ang = positions.astype(F32).reshape(-1)[:, None] * inv_freq
    cos, sin = jnp.cos(ang), jnp.sin(ang)
    t = cos.shape[0]
    one = jnp.ones((t, head_dim - rot_dim), F32)
    zero = jnp.zeros((t, head_dim - rot_dim), F32)
    zh = jnp.zeros((t, half), F32)
    c = jnp.concatenate([cos, cos, one], axis=1)
    s1 = jnp.concatenate([zh, sin, zero], axis=1)
    s2 = jnp.concatenate([-sin, zh, zero], axis=1)
    reps = active_lanes // head_dim

    def widen(a, fill):
        a = jnp.tile(a, (1, reps))
        return jnp.concatenate([a, jnp.full((t, LANES - active_lanes), fill, F32)], axis=1)

    return jnp.concatenate([widen(c, 1.0), widen(s1, 0.0), widen(s2, 0.0)], axis=1)


def _route(idx, t, n_tiles):
    flat_e = idx.reshape(-1)
    onehot = (flat_e[:, None] == jnp.arange(N_EXPERTS, dtype=I32)[None, :]).astype(I32)
    csum = jnp.cumsum(onehot, axis=0)
    rank = jnp.take_along_axis(csum, flat_e[:, None], axis=1)[:, 0] - 1
    counts = csum[-1]
    padded = (counts + EXPERT_BLOCK - 1) // EXPERT_BLOCK * EXPERT_BLOCK
    pend = jnp.cumsum(padded)
    pstart = pend - padded
    dest = pstart[flat_e] + rank
    cap = n_tiles * EXPERT_BLOCK
    buf_tok = (jnp.arange(cap, dtype=I32) % t).at[dest].set(jnp.arange(flat_e.shape[0], dtype=I32) // TOP_K_EXPERTS)
    tile_start = jnp.arange(n_tiles, dtype=I32) * EXPERT_BLOCK
    tile_e = jnp.minimum(jnp.sum((pend[None, :] <= tile_start[:, None]).astype(I32), axis=1), N_EXPERTS - 1)
    n_used = (pend[-1] // EXPERT_BLOCK).astype(I32).reshape(1)
    return dest, buf_tok, tile_e, n_used


def kernel(x, mem, positions, norm_mix, w_in, conv_w, w_out, norm_xattn, norm_mem, wq_x, wk_x, wv_x, wo_x,
           norm_ffn, w_router, b_router, w_gate_up, b_gate_up, w_down, b_down, norm_final):
    b, s, d = x.shape
    t = b * s
    assert d == N_HEADS * HEAD_DIM and s % 512 == 0
    x2 = x.reshape(t, d)
    h = x2
    for l in range(w_in.shape[0]):
        wl = w_in[l]
        o = [0]
        for n in (d, d, d, d, N_KV_HEADS * HEAD_DIM, N_KV_HEADS * HEAD_DIM, IDX_HEADS * IDX_DIM, IDX_DIM, IDX_HEADS, d, d):
            o.append(o[-1] + n)
        seg = lambda a: wl[:, o[a]:o[a + 1]]
        w_a = jnp.concatenate([seg(0), seg(1), seg(2), seg(9), seg(10), seg(5)], axis=1).astype(BF16)
        pad_b = jnp.zeros((d, LANES - IDX_DIM - IDX_HEADS + 128), F32)
        w_b = jnp.concatenate([seg(3), seg(6), seg(4), seg(7), seg(8), pad_b], axis=1).astype(BF16)
        za = norm_matmul(h, norm_mix[l], w_a, BF16, 1024, 1536)
        zb = norm_matmul(h, norm_mix[l], w_b, F32, 1024, 1280)
        tq = _rope_table(positions, ROT_DIM, HEAD_DIM, LANES)
        ti = _rope_table(positions, IDX_ROT_DIM, IDX_DIM, LANES)
        tk = _rope_table(positions, IDX_ROT_DIM, IDX_DIM, IDX_DIM)
        tq_rows = 512
        q_r, qi_r, k_r, ki_r, wi_r = rope_split(zb, tq, ti, tk, b, s, tq_rows)
        topk = min(TOPK_MAX, s // 4)
        tkc = 512
        wi_t = wi_r.reshape(b, s // Q_BLOCK, Q_BLOCK, IDX_HEADS).transpose(0, 1, 3, 2)
        bias = index_select(qi_r, ki_r, wi_t, b, s, topk, tkc)
        y_attn = masked_attention(q_r, k_r, za, 5 * d // HEAD_DIM, bias, b, s, tq_rows, tkc)
        h = merge_outproj(za, y_attn, conv_w[l], w_out[l].astype(BF16), h, s, 256)
        kv_w = jnp.concatenate([wk_x[l], wv_x[l]], axis=1).astype(BF16)
        kv = norm_matmul(mem.reshape(-1, d), norm_mem[l], kv_w, BF16, 1024, 512)
        h, hf, idx, gates = cross_attention_router(h, norm_xattn[l], wq_x[l].astype(BF16), kv, wo_x[l].astype(BF16),
                                                   norm_ffn[l], w_router[l], b_router[l], s, 256)
        n_tiles = -(-(t * TOP_K_EXPERTS) // EXPERT_BLOCK) + N_EXPERTS
        dest, buf_tok, tile_e, n_used = _route(idx[:, :TOP_K_EXPERTS], t, n_tiles)
        xs = hf[buf_tok]
        y = expert_ffn(xs, tile_e, n_used, w_gate_up[l], b_gate_up[l], w_down[l], b_down[l], 1024, 2048)
        dest2 = dest.reshape(t, TOP_K_EXPERTS)
        ys = [y[dest2[:, kk]] for kk in range(TOP_K_EXPERTS)]
        h = combine_norm(h, ys, gates, norm_final, 256, final=(l + 1 == w_in.shape[0]))
    return h.reshape(b, s, d)
```

```python
import functools

import jax
import jax.numpy as jnp
from jax import lax
from jax.experimental import pallas as pl
from jax.experimental.pallas import tpu as pltpu

F32 = jnp.float32
BF16 = jnp.bfloat16
I32 = jnp.int32

HEAD_DIM = 128
N_HEADS = 16
N_KV_HEADS = 4
GROUP = N_HEADS // N_KV_HEADS
ROT_DIM = 32
IDX_HEADS = 16
IDX_DIM = 64
IDX_ROT_DIM = 16
TOPK_MAX = 256
Q_BLOCK = 128
ROPE_THETA = 500000.0
X_HEADS = 4
X_HEAD_DIM = 128
N_EXPERTS = 32
TOP_K_EXPERTS = 4
SWIGLU_LIMIT = 7.0
SWIGLU_ALPHA = 1.702
EXPERT_BLOCK = 512
NORM_EPS = 1e-6
CONV_WIDTH = 3

LANES = 128
NEG = -1e30
INT_MIN = -2 ** 31
MIB = 1024 * 1024
LOG2E = 1.4426950408889634


def _params(sem, vmem_mib):
    return pltpu.CompilerParams(dimension_semantics=sem, vmem_limit_bytes=vmem_mib * MIB)


def _rms(x, g):
    ms = jnp.mean(x * x, axis=-1, keepdims=True)
    return x * lax.rsqrt(ms + NORM_EPS) * g


def _norm_matmul_body(x_ref, g_ref, w_ref, o_ref, a_scr):
    @pl.when(pl.program_id(1) == 0)
    def _():
        a_scr[...] = _rms(x_ref[...], g_ref[...]).astype(BF16)

    o_ref[...] = jnp.dot(a_scr[...], w_ref[...], preferred_element_type=F32).astype(o_ref.dtype)


def norm_matmul(x, g, w, out_dtype, tm, tn):
    m, k = x.shape
    n = w.shape[1]
    tm = min(tm, m)
    assert m % tm == 0 and n % tn == 0
    return pl.pallas_call(
        _norm_matmul_body,
        grid=(m // tm, n // tn),
        in_specs=[pl.BlockSpec((tm, k), lambda i, j: (i, 0)),
                  pl.BlockSpec((1, k), lambda i, j: (0, 0)),
                  pl.BlockSpec((k, tn), lambda i, j: (0, j))],
        out_specs=pl.BlockSpec((tm, tn), lambda i, j: (i, j)),
        out_shape=jax.ShapeDtypeStruct((m, n), out_dtype),
        scratch_shapes=[pltpu.VMEM((tm, k), BF16)],
        compiler_params=_params(("parallel", "arbitrary"), 48),
        name="norm_matmul",
    )(x, g.reshape(1, k), w)


def _rope(x, tab, shift):
    c, s1, s2 = tab[:, 0:LANES], tab[:, LANES:2 * LANES], tab[:, 2 * LANES:3 * LANES]
    return x * c + pltpu.roll(x, shift, 1) * s1 + pltpu.roll(x, LANES - shift, 1) * s2


def _rope_body(q_ref, qi_ref, k_ref, kw_ref, tq_ref, ti_ref, tk_ref,
               qo_ref, qio_ref, ko_ref, kio_ref, wio_ref, *, tm):
    tq = tq_ref[...]
    scale = HEAD_DIM ** -0.5 * LOG2E
    for h in range(N_HEADS):
        xh = q_ref[:, h * LANES:(h + 1) * LANES]
        xr = (_rope(xh, tq, ROT_DIM // 2) * scale).astype(BF16)
        for qb in range(tm // Q_BLOCK):
            qo_ref[0, qb, h * Q_BLOCK:(h + 1) * Q_BLOCK, :] = xr[qb * Q_BLOCK:(qb + 1) * Q_BLOCK]
    for g in range(N_KV_HEADS):
        xh = k_ref[:, g * LANES:(g + 1) * LANES]
        ko_ref[:, g * LANES:(g + 1) * LANES] = _rope(xh, tq, ROT_DIM // 2).astype(BF16)
    ti = ti_ref[...]
    for j in range(IDX_HEADS * IDX_DIM // LANES):
        xr = _rope(qi_ref[:, j * LANES:(j + 1) * LANES], ti, IDX_ROT_DIM // 2).astype(BF16)
        for half in range(LANES // IDX_DIM):
            h = j * (LANES // IDX_DIM) + half
            part = xr[:, half * IDX_DIM:(half + 1) * IDX_DIM]
            for qb in range(tm // Q_BLOCK):
                qio_ref[0, qb, h * Q_BLOCK:(h + 1) * Q_BLOCK, :] = part[qb * Q_BLOCK:(qb + 1) * Q_BLOCK]
    kw = kw_ref[...]
    kr = _rope(kw, tk_ref[...], IDX_ROT_DIM // 2)
    kio_ref[...] = kr[:, 0:IDX_DIM].astype(BF16)
    wio_ref[...] = kw[:, IDX_DIM:IDX_DIM + IDX_HEADS] * (IDX_HEADS ** -0.5 * IDX_DIM ** -0.5)


def rope_split(zb, tq, ti, tk, b, s, tm):
    t = b * s
    nq = s // tm
    d = N_HEADS * HEAD_DIM
    qi_w = IDX_HEADS * IDX_DIM
    kv_w = N_KV_HEADS * HEAD_DIM
    off_qi = d // qi_w
    off_k = (d + qi_w) // kv_w
    off_kw = (d + qi_w + kv_w) // LANES
    row = lambda bb, i: bb * nq + i
    return pl.pallas_call(
        functools.partial(_rope_body, tm=tm),
        grid=(b, nq),
        in_specs=[pl.BlockSpec((tm, d), lambda bb, i: (row(bb, i), 0)),
                  pl.BlockSpec((tm, qi_w), lambda bb, i: (row(bb, i), off_qi)),
                  pl.BlockSpec((tm, kv_w), lambda bb, i: (row(bb, i), off_k)),
                  pl.BlockSpec((tm, LANES), lambda bb, i: (row(bb, i), off_kw)),
                  pl.BlockSpec((tm, 3 * LANES), lambda bb, i: (row(bb, i), 0)),
                  pl.BlockSpec((tm, 3 * LANES), lambda bb, i: (row(bb, i), 0)),
                  pl.BlockSpec((tm, 3 * LANES), lambda bb, i: (row(bb, i), 0))],
        out_specs=[pl.BlockSpec((1, tm // Q_BLOCK, N_HEADS * Q_BLOCK, HEAD_DIM), lambda bb, i: (bb, i, 0, 0)),
                   pl.BlockSpec((1, tm // Q_BLOCK, IDX_HEADS * Q_BLOCK, IDX_DIM), lambda bb, i: (bb, i, 0, 0)),
                   pl.BlockSpec((tm, kv_w), lambda bb, i: (row(bb, i), 0)),
                   pl.BlockSpec((tm, IDX_DIM), lambda bb, i: (row(bb, i), 0)),
                   pl.BlockSpec((tm, IDX_HEADS), lambda bb, i: (row(bb, i), 0))],
        out_shape=[jax.ShapeDtypeStruct((b, s // Q_BLOCK, N_HEADS * Q_BLOCK, HEAD_DIM), BF16),
                   jax.ShapeDtypeStruct((b, s // Q_BLOCK, IDX_HEADS * Q_BLOCK, IDX_DIM), BF16),
                   jax.ShapeDtypeStruct((t, kv_w), BF16),
                   jax.ShapeDtypeStruct((t, IDX_DIM), BF16),
                   jax.ShapeDtypeStruct((t, IDX_HEADS), F32)],
        compiler_params=_params(("parallel", "parallel"), 48),
        name="rope_split",
    )(zb, zb, zb, zb, tq, ti, tk)


def _index_body(qi_ref, ki_ref, wi_ref, o_ref, skey_ref, tie_ref, *, topk, ch):
    i = pl.program_id(1)
    nkc = skey_ref.shape[0]
    pos_bits = (nkc * ch).bit_length()
    nch = (i * Q_BLOCK + Q_BLOCK + ch - 1) // ch
    qi = qi_ref[0, 0]
    wi = wi_ref[0, 0]
    key_pos = lax.broadcasted_iota(I32, (ch, Q_BLOCK), 0)
    q_pos = i * Q_BLOCK + lax.broadcasted_iota(I32, (ch, Q_BLOCK), 1)

    def score_chunk(c, carry):
        kc = ki_ref[pl.ds(pl.multiple_of(c * ch, ch), ch), :]
        logits = lax.dot_general(kc, qi, (((1,), (1,)), ((), ())), preferred_element_type=F32)
        acc = jnp.zeros((ch, Q_BLOCK), F32)
        for h in range(IDX_HEADS):
            acc = acc + jnp.maximum(logits[:, h * Q_BLOCK:(h + 1) * Q_BLOCK], 0.0) * wi[h:h + 1, :]
        bits = lax.bitcast_convert_type(acc, I32)
        key = bits ^ ((bits >> 31) & 0x7FFFFFFF)
        skey_ref[c] = jnp.where(c * ch + key_pos <= q_pos, key, INT_MIN)
        return carry

    lax.fori_loop(0, nch, score_chunk, 0)

    def count(pred):
        def body(c, cnt):
            m = jnp.where(pred(c, skey_ref[c]), 1.0, 0.0)
            return cnt + jnp.sum(m.reshape(ch // 64, 64, Q_BLOCK), axis=0)
        cnt = lax.fori_loop(0, nch, body, jnp.zeros((64, Q_BLOCK), F32))
        return jnp.sum(cnt, axis=0, keepdims=True)

    def bit_body(t, u):
        cand_u = u | jnp.left_shift(jnp.int32(1), 31 - t)
        cand = cand_u ^ INT_MIN
        cnt = count(lambda c, keys: keys >= cand)
        return jnp.where(cnt >= topk, cand_u, u)

    u = lax.fori_loop(0, 32, bit_body, jnp.zeros((1, Q_BLOCK), I32))
    thr = jnp.maximum(u ^ INT_MIN, INT_MIN + 1)

    n_ge = count(lambda c, keys: keys >= thr)
    tie_ref[...] = jnp.full(tie_ref.shape, nkc * ch, I32)

    @pl.when(jnp.max(n_ge) > topk)
    def _():
        need = topk - count(lambda c, keys: keys > thr)

        def pos_body(t, end):
            cand = end + jnp.left_shift(jnp.int32(1), pos_bits - 1 - t)
            cnt = count(lambda c, keys: (keys == thr) & (c * ch + key_pos < cand))
            return jnp.where(cnt <= need, cand, end)

        end = lax.fori_loop(0, pos_bits, pos_body, jnp.zeros((1, Q_BLOCK), I32))
        tie_ref[...] = jnp.broadcast_to(jnp.where(n_ge > topk, end, nkc * ch), tie_ref.shape)

    tie_end = tie_ref[0:1, :]
    def fill_chunk(c, carry):
        o_ref[0, 0, c] = jnp.full((ch, Q_BLOCK), NEG, o_ref.dtype)
        return carry

    lax.fori_loop(nch, nkc, fill_chunk, 0)

    def write_chunk(c, carry):
        keys = skey_ref[c]
        keep = (keys > thr) | ((keys == thr) & (c * ch + key_pos < tie_end))
        o_ref[0, 0, c] = jnp.where(keep, 0.0, NEG).astype(o_ref.dtype)
        return carry

    lax.fori_loop(0, nch, write_chunk, 0)


def index_select(qi, ki, wi_t, b, s, topk, ch):
    nqb = s // Q_BLOCK
    nkc = s // ch
    return pl.pallas_call(
        functools.partial(_index_body, topk=topk, ch=ch),
        grid=(b, nqb),
        in_specs=[pl.BlockSpec((1, 1, IDX_HEADS * Q_BLOCK, IDX_DIM), lambda bb, i: (bb, i, 0, 0)),
                  pl.BlockSpec((s, IDX_DIM), lambda bb, i: (bb, 0)),
                  pl.BlockSpec((1, 1, IDX_HEADS, Q_BLOCK), lambda bb, i: (bb, i, 0, 0))],
        out_specs=pl.BlockSpec((1, 1, nkc, ch, Q_BLOCK), lambda bb, i: (bb, i, 0, 0, 0)),
        out_shape=jax.ShapeDtypeStruct((b, nqb, nkc, ch, Q_BLOCK), BF16),
        scratch_shapes=[pltpu.VMEM((nkc, ch, Q_BLOCK), I32), pltpu.VMEM((8, Q_BLOCK), I32)],
        compiler_params=_params(("parallel", "arbitrary"), 48),
        name="index_select",
    )(qi, ki, wi_t)


def _attn_body(pi_ref, pk_ref, q_ref, k_ref, v_ref, b_ref, o_ref, lhs_scr, m_scr, acc_scr, *, tq, tk):
    step = pl.program_id(2)
    i = pi_ref[step]
    kc = pk_ref[step]
    last = ((i + 1) * tq - 1) // tk
    nqb = tq // Q_BLOCK
    rows = GROUP * Q_BLOCK

    @pl.when(kc == 0)
    def _():
        r = lax.broadcasted_iota(I32, (rows, Q_BLOCK), 0)
        c = lax.broadcasted_iota(I32, (rows, Q_BLOCK), 1)
        eye = jnp.where(r % Q_BLOCK == c, 1.0, 0.0).astype(BF16)
        for j in range(nqb):
            lhs_scr[j] = jnp.concatenate([q_ref[0, j], eye], axis=1)
        m_scr[...] = jnp.full(m_scr.shape, NEG, F32)
        acc_scr[...] = jnp.zeros(acc_scr.shape, F32)

    k = k_ref[...]
    v_one = jnp.concatenate([v_ref[...], jnp.ones((tk, LANES), BF16)], axis=1)
    for j in range(nqb):
        rhs = jnp.concatenate([k, b_ref[0, j, 0]], axis=1)
        s = lax.dot_general(lhs_scr[j], rhs, (((1,), (1,)), ((), ())), preferred_element_type=F32)
        m_prev = m_scr[j]
        m_new = jnp.maximum(m_prev, jnp.max(s, axis=1, keepdims=True))
        alpha = jnp.exp2(m_prev - m_new)
        p = jnp.exp2(s - jnp.concatenate([m_new] * (tk // LANES), axis=1))
        acc_scr[j] = (jnp.concatenate([alpha, alpha], axis=1) * acc_scr[j]
                      + jnp.dot(p.astype(BF16), v_one, preferred_element_type=F32))
        m_scr[j] = m_new

    @pl.when(kc == last)
    def _():
        for j in range(nqb):
            acc = acc_scr[j]
            o = acc[:, 0:HEAD_DIM] / acc[:, HEAD_DIM:2 * HEAD_DIM]
            for h in range(GROUP):
                o_ref[j * Q_BLOCK:(j + 1) * Q_BLOCK, h * HEAD_DIM:(h + 1) * HEAD_DIM] = (
                    o[h * Q_BLOCK:(h + 1) * Q_BLOCK].astype(o_ref.dtype))


def masked_attention(q, k, za, v_col0, bias, b, s, tq, tk):
    nqt = s // tq
    nkc = s // tk
    t = b * s
    pairs = [(i, kc) for i in range(nqt) for kc in range(((i + 1) * tq - 1) // tk + 1)]
    pi = jnp.asarray([p[0] for p in pairs], I32)
    pk = jnp.asarray([p[1] for p in pairs], I32)
    return pl.pallas_call(
        functools.partial(_attn_body, tq=tq, tk=tk),
        grid_spec=pltpu.PrefetchScalarGridSpec(
            num_scalar_prefetch=2,
            grid=(b, N_KV_HEADS, len(pairs)),
            in_specs=[pl.BlockSpec((1, tq // Q_BLOCK, GROUP * Q_BLOCK, HEAD_DIM),
                                   lambda bb, g, p, pi, pk: (bb, pi[p], g, 0)),
                      pl.BlockSpec((tk, HEAD_DIM), lambda bb, g, p, pi, pk: (bb * nkc + pk[p], g)),
                      pl.BlockSpec((tk, HEAD_DIM), lambda bb, g, p, pi, pk: (bb * nkc + pk[p], v_col0 + g)),
                      pl.BlockSpec((1, tq // Q_BLOCK, 1, tk, Q_BLOCK),
                                   lambda bb, g, p, pi, pk: (bb, pi[p], pk[p], 0, 0))],
            out_specs=pl.BlockSpec((tq, GROUP * HEAD_DIM), lambda bb, g, p, pi, pk: (bb * nqt + pi[p], g)),
            scratch_shapes=[pltpu.VMEM((tq // Q_BLOCK, GROUP * Q_BLOCK, 2 * HEAD_DIM), BF16),
                            pltpu.VMEM((tq // Q_BLOCK, GROUP * Q_BLOCK, LANES), F32),
                            pltpu.VMEM((tq // Q_BLOCK, GROUP * Q_BLOCK, 2 * HEAD_DIM), F32)]),
        out_shape=jax.ShapeDtypeStruct((t, N_HEADS * HEAD_DIM), BF16),
        compiler_params=_params(("parallel", "parallel", "arbitrary"), 48),
        name="masked_attention",
    )(pi, pk, q, k, za, bias)


def _merge_body(cb_ref, cc_ref, cx_ref, gc_ref, ga_ref, cch_ref, cxh_ref, ya_ref, cw_ref, wo_ref, x_ref,
                o_ref, *, tm, tiles_per_seq):
    i = pl.program_id(0)
    p = cc_ref[...].astype(F32) * cx_ref[...].astype(F32)
    halo = cch_ref[...].astype(F32) * cxh_ref[...].astype(F32)
    halo = jnp.where(i % tiles_per_seq == 0, 0.0, halo)
    row = lax.broadcasted_iota(I32, p.shape, 0)
    p1 = jnp.where(row == 0, halo[7:8], pltpu.roll(p, 1, 0))
    p2 = jnp.where(row == 0, halo[6:7], jnp.where(row == 1, halo[7:8], pltpu.roll(p, 2, 0)))
    cw = cw_ref[...]
    conv = cw[0:1] * p2 + cw[1:2] * p1 + cw[2:3] * p
    merged = (jax.nn.sigmoid(gc_ref[...].astype(F32)) * (cb_ref[...].astype(F32) * conv)
              + jax.nn.sigmoid(ga_ref[...].astype(F32)) * ya_ref[...].astype(F32))
    o_ref[...] = x_ref[...] + jnp.dot(merged.astype(BF16), wo_ref[...], preferred_element_type=F32)


def merge_outproj(za, y_attn, conv_w, w_out, x, s, tm):
    t, c = x.shape
    hb = tm // 8
    col = lambda j: (lambda i: (i, j))
    halo = lambda j: (lambda i: (jnp.maximum(i * hb - 1, 0), j))
    return pl.pallas_call(
        functools.partial(_merge_body, tm=tm, tiles_per_seq=s // tm),
        grid=(t // tm,),
        in_specs=[pl.BlockSpec((tm, c), col(0)), pl.BlockSpec((tm, c), col(1)), pl.BlockSpec((tm, c), col(2)),
                  pl.BlockSpec((tm, c), col(3)), pl.BlockSpec((tm, c), col(4)),
                  pl.BlockSpec((8, c), halo(1)), pl.BlockSpec((8, c), halo(2)),
                  pl.BlockSpec((tm, c), lambda i: (i, 0)),
                  pl.BlockSpec((CONV_WIDTH, c), lambda i: (0, 0)),
                  pl.BlockSpec((c, c), lambda i: (0, 0)),
                  pl.BlockSpec((tm, c), lambda i: (i, 0))],
        out_specs=pl.BlockSpec((tm, c), lambda i: (i, 0)),
        out_shape=jax.ShapeDtypeStruct((t, c), F32),
        compiler_params=_params(("parallel",), 56),
        name="merge_outproj",
    )(za, za, za, za, za, za, za, y_attn, conv_w, w_out, x)


def _xattn_body(h_ref, g_ref, wq_ref, kv_ref, wo_ref, o_ref):
    h = h_ref[...]
    hn = _rms(h, g_ref[...]).astype(BF16)
    q = jnp.dot(hn, wq_ref[...], preferred_element_type=F32) * (X_HEAD_DIM ** -0.5)
    q = q.astype(BF16)
    kv = kv_ref[...]
    kw = X_HEADS * X_HEAD_DIM
    outs = []
    for hh in range(X_HEADS):
        sl = slice(hh * X_HEAD_DIM, (hh + 1) * X_HEAD_DIM)
        s = lax.dot_general(q[:, sl], kv[:, sl], (((1,), (1,)), ((), ())), preferred_element_type=F32)
        m = jnp.max(s, axis=1, keepdims=True)
        p = jnp.exp(s - m)
        l = jnp.sum(p, axis=1, keepdims=True)
        vh = kv[:, kw + hh * X_HEAD_DIM: kw + (hh + 1) * X_HEAD_DIM]
        outs.append((jnp.dot(p.astype(BF16), vh, preferred_element_type=F32) / l).astype(BF16))
    o = jnp.concatenate(outs, axis=1)
    o_ref[...] = h + jnp.dot(o, wo_ref[...], preferred_element_type=F32)


def cross_attention(h, g, wq, kv, wo, s, tm):
    t, d = h.shape
    m = kv.shape[0] // (t // s)
    kw = X_HEADS * X_HEAD_DIM
    tps = s // tm
    return pl.pallas_call(
        _xattn_body,
        grid=(t // tm,),
        in_specs=[pl.BlockSpec((tm, d), lambda i: (i, 0)),
                  pl.BlockSpec((1, d), lambda i: (0, 0)),
                  pl.BlockSpec((d, kw), lambda i: (0, 0)),
                  pl.BlockSpec((m, 2 * kw), lambda i: (i // tps, 0)),
                  pl.BlockSpec((kw, d), lambda i: (0, 0))],
        out_specs=pl.BlockSpec((tm, d), lambda i: (i, 0)),
        out_shape=jax.ShapeDtypeStruct((t, d), F32),
        compiler_params=_params(("parallel",), 48),
        name="cross_attention",
    )(h, g.reshape(1, d), wq, kv, wo)


def _router_body(h_ref, g_ref, wr_ref, br_ref, hf_ref, idx_ref, gate_ref):
    hf = _rms(h_ref[...], g_ref[...])
    hf_ref[...] = hf.astype(BF16)
    logits = jnp.dot(hf, wr_ref[...], preferred_element_type=F32, precision=lax.Precision.HIGHEST) + br_ref[...]
    lane = lax.broadcasted_iota(I32, logits.shape, 1)
    lane_f = lane.astype(F32)
    l = jnp.where(lane < N_EXPERTS, logits, -jnp.inf)
    vals, idxs = [], []
    for _ in range(TOP_K_EXPERTS):
        m = jnp.max(l, axis=1, keepdims=True)
        ix = jnp.min(jnp.where(l == m, lane_f, float(LANES)), axis=1, keepdims=True)
        vals.append(m)
        idxs.append(ix)
        l = jnp.where(lane_f == ix, -jnp.inf, l)
    es = [jnp.exp(v - vals[0]) for v in vals]
    den = es[0] + es[1] + es[2] + es[3]
    idx_o = jnp.zeros(logits.shape, F32)
    gate_o = jnp.zeros(logits.shape, F32)
    for kk in range(TOP_K_EXPERTS):
        idx_o = jnp.where(lane == kk, idxs[kk], idx_o)
        gate_o = jnp.where(lane == kk, es[kk] / den, gate_o)
    idx_ref[...] = idx_o.astype(I32)
    gate_ref[...] = gate_o


def ffn_router(h, g, w_router, b_router, tm):
    t, d = h.shape
    wr = jnp.zeros((d, LANES), F32).at[:, :N_EXPERTS].set(w_router)
    br = jnp.zeros((1, LANES), F32).at[0, :N_EXPERTS].set(b_router)
    return pl.pallas_call(
        _router_body,
        grid=(t // tm,),
        in_specs=[pl.BlockSpec((tm, d), lambda i: (i, 0)),
                  pl.BlockSpec((1, d), lambda i: (0, 0)),
                  pl.BlockSpec((d, LANES), lambda i: (0, 0)),
                  pl.BlockSpec((1, LANES), lambda i: (0, 0))],
        out_specs=[pl.BlockSpec((tm, d), lambda i: (i, 0)),
                   pl.BlockSpec((tm, LANES), lambda i: (i, 0)),
                   pl.BlockSpec((tm, LANES), lambda i: (i, 0))],
        out_shape=[jax.ShapeDtypeStruct((t, d), BF16),
                   jax.ShapeDtypeStruct((t, LANES), I32),
                   jax.ShapeDtypeStruct((t, LANES), F32)],
        compiler_params=_params(("parallel",), 48),
        name="ffn_router",
    )(h, g.reshape(1, d), wr, br)


def _expert_changed(te_ref, i):
    return (i == 0) | (te_ref[i] != te_ref[jnp.maximum(i - 1, 0)])


def _gmm1_body(te_ref, nu_ref, x_ref, wg_ref, wu_ref, bg_ref, bu_ref, o_ref, wg_s, wu_s):
    i = pl.program_id(1)

    @pl.when(i < nu_ref[0])
    def _():
        @pl.when(_expert_changed(te_ref, i))
        def _():
            wg_s[...] = wg_ref[0].astype(BF16)
            wu_s[...] = wu_ref[0].astype(BF16)

        x = x_ref[...]
        gate = jnp.dot(x, wg_s[...], preferred_element_type=F32) + bg_ref[0]
        up = jnp.dot(x, wu_s[...], preferred_element_type=F32) + bu_ref[0]
        gate = jnp.minimum(gate, SWIGLU_LIMIT)
        up = jnp.clip(up, -SWIGLU_LIMIT, SWIGLU_LIMIT)
        o_ref[...] = ((up + 1.0) * (gate * jax.nn.sigmoid(SWIGLU_ALPHA * gate))).astype(o_ref.dtype)


def _gmm2_body(te_ref, nu_ref, h_ref, wd_ref, bd_ref, o_ref, wd_s):
    i = pl.program_id(1)

    @pl.when(i < nu_ref[0])
    def _():
        @pl.when(_expert_changed(te_ref, i))
        def _():
            wd_s[...] = wd_ref[0].astype(BF16)

        o_ref[...] = (jnp.dot(h_ref[...], wd_s[...], preferred_element_type=F32) + bd_ref[0]).astype(o_ref.dtype)


def expert_ffn(xs, tile_e, n_used, w_gu, b_gu, w_dn, b_dn, tf, tn):
    cap, d = xs.shape
    dff = w_dn.shape[1]
    n_tiles = cap // EXPERT_BLOCK
    nj = dff // tf
    tile = lambda i, nu: jnp.minimum(i, nu[0] - 1)
    b_gu3 = b_gu.reshape(N_EXPERTS, 1, 2 * dff)
    b_dn3 = b_dn.reshape(N_EXPERTS, 1, d)
    hdn = pl.pallas_call(
        _gmm1_body,
        grid_spec=pltpu.PrefetchScalarGridSpec(
            num_scalar_prefetch=2,
            grid=(nj, n_tiles),
            in_specs=[pl.BlockSpec((EXPERT_BLOCK, d), lambda j, i, te, nu: (tile(i, nu), 0)),
                      pl.BlockSpec((1, d, tf), lambda j, i, te, nu: (te[tile(i, nu)], 0, j)),
                      pl.BlockSpec((1, d, tf), lambda j, i, te, nu: (te[tile(i, nu)], 0, nj + j)),
                      pl.BlockSpec((1, 1, tf), lambda j, i, te, nu: (te[tile(i, nu)], 0, j)),
                      pl.BlockSpec((1, 1, tf), lambda j, i, te, nu: (te[tile(i, nu)], 0, nj + j))],
            out_specs=pl.BlockSpec((EXPERT_BLOCK, tf), lambda j, i, te, nu: (tile(i, nu), j)),
            scratch_shapes=[pltpu.VMEM((d, tf), BF16), pltpu.VMEM((d, tf), BF16)]),
        out_shape=jax.ShapeDtypeStruct((cap, dff), BF16),
        compiler_params=_params(("arbitrary", "arbitrary"), 60),
        name="expert_gate_up",
    )(tile_e, n_used, xs, w_gu, w_gu, b_gu3, b_gu3)
    nn = d // tn
    return pl.pallas_call(
        _gmm2_body,
        grid_spec=pltpu.PrefetchScalarGridSpec(
            num_scalar_prefetch=2,
            grid=(nn, n_tiles),
            in_specs=[pl.BlockSpec((EXPERT_BLOCK, dff), lambda j, i, te, nu: (tile(i, nu), 0)),
                      pl.BlockSpec((1, dff, tn), lambda j, i, te, nu: (te[tile(i, nu)], 0, j)),
                      pl.BlockSpec((1, 1, tn), lambda j, i, te, nu: (te[tile(i, nu)], 0, j))],
            out_specs=pl.BlockSpec((EXPERT_BLOCK, tn), lambda j, i, te, nu: (tile(i, nu), j)),
            scratch_shapes=[pltpu.VMEM((dff, tn), BF16)]),
        out_shape=jax.ShapeDtypeStruct((cap, d), BF16),
        compiler_params=_params(("arbitrary", "arbitrary"), 60),
        name="expert_down",
    )(tile_e, n_used, hdn, w_dn, b_dn3)


def _combine_body(h_ref, y0_ref, y1_ref, y2_ref, y3_ref, gate_ref, g_ref, o_ref, *, final):
    acc = h_ref[...]
    gate = gate_ref[...]
    for kk, y_ref in enumerate((y0_ref, y1_ref, y2_ref, y3_ref)):
        acc = acc + gate[:, kk:kk + 1] * y_ref[...].astype(F32)
    o_ref[...] = _rms(acc, g_ref[...]) if final else acc


def combine_norm(h, ys, gates, g, tm, final):
    t, d = h.shape
    return pl.pallas_call(
        functools.partial(_combine_body, final=final),
        grid=(t // tm,),
        in_specs=[pl.BlockSpec((tm, d), lambda i: (i, 0))] * (1 + TOP_K_EXPERTS)
                 + [pl.BlockSpec((tm, LANES), lambda i: (i, 0)),
                    pl.BlockSpec((1, d), lambda i: (0, 0))],
        out_specs=pl.BlockSpec((tm, d), lambda i: (i, 0)),
        out_shape=jax.ShapeDtypeStruct((t, d), F32),
        compiler_params=_params(("parallel",), 48),
        name="combine_norm",
    )(h, *ys, gates, g.reshape(1, d))


def _rope_table(positions, rot_dim, head_dim, active_lanes):
    half = rot_dim // 2
    inv_freq = jnp.float32(ROPE_THETA) ** (-(jnp.arange(half, dtype=F32) * 2.0 / rot_dim))
    ang = positions.astype(F32).reshape(-1)[:, None] * inv_freq
    cos, sin = jnp.cos(ang), jnp.sin(ang)
    t = cos.shape[0]
    one = jnp.ones((t, head_dim - rot_dim), F32)
    zero = jnp.zeros((t, head_dim - rot_dim), F32)
    zh = jnp.zeros((t, half), F32)
    c = jnp.concatenate([cos, cos, one], axis=1)
    s1 = jnp.concatenate([zh, sin, zero], axis=1)
    s2 = jnp.concatenate([-sin, zh, zero], axis=1)
    reps = active_lanes // head_dim

    def widen(a, fill):
        a = jnp.tile(a, (1, reps))
        return jnp.concatenate([a, jnp.full((t, LANES - active_lanes), fill, F32)], axis=1)

    return jnp.concatenate([widen(c, 1.0), widen(s1, 0.0), widen(s2, 0.0)], axis=1)


def _route(idx, t, n_tiles):
    flat_e = idx.reshape(-1)
    onehot = (flat_e[:, None] == jnp.arange(N_EXPERTS, dtype=I32)[None, :]).astype(I32)
    csum = jnp.cumsum(onehot, axis=0)
    rank = jnp.take_along_axis(csum, flat_e[:, None], axis=1)[:, 0] - 1
    counts = csum[-1]
    padded = (counts + EXPERT_BLOCK - 1) // EXPERT_BLOCK * EXPERT_BLOCK
    pend = jnp.cumsum(padded)
    pstart = pend - padded
    dest = pstart[flat_e] + rank
    cap = n_tiles * EXPERT_BLOCK
    buf_tok = (jnp.arange(cap, dtype=I32) % t).at[dest].set(jnp.arange(flat_e.shape[0], dtype=I32) // TOP_K_EXPERTS)
    tile_start = jnp.arange(n_tiles, dtype=I32) * EXPERT_BLOCK
    tile_e = jnp.minimum(jnp.sum((pend[None, :] <= tile_start[:, None]).astype(I32), axis=1), N_EXPERTS - 1)
    n_used = (pend[-1] // EXPERT_BLOCK).astype(I32).reshape(1)
    return dest, buf_tok, tile_e, n_used


def kernel(x, mem, positions, norm_mix, w_in, conv_w, w_out, norm_xattn, norm_mem, wq_x, wk_x, wv_x, wo_x,
           norm_ffn, w_router, b_router, w_gate_up, b_gate_up, w_down, b_down, norm_final):
    b, s, d = x.shape
    t = b * s
    assert d == N_HEADS * HEAD_DIM and s % 512 == 0
    x2 = x.reshape(t, d)
    h = x2
    for l in range(w_in.shape[0]):
        wl = w_in[l]
        o = [0]
        for n in (d, d, d, d, N_KV_HEADS * HEAD_DIM, N_KV_HEADS * HEAD_DIM, IDX_HEADS * IDX_DIM, IDX_DIM, IDX_HEADS, d, d):
            o.append(o[-1] + n)
        seg = lambda a: wl[:, o[a]:o[a + 1]]
        w_a = jnp.concatenate([seg(0), seg(1), seg(2), seg(9), seg(10), seg(5)], axis=1).astype(BF16)
        pad_b = jnp.zeros((d, LANES - IDX_DIM - IDX_HEADS + 128), F32)
        w_b = jnp.concatenate([seg(3), seg(6), seg(4), seg(7), seg(8), pad_b], axis=1).astype(BF16)
        za = norm_matmul(h, norm_mix[l], w_a, BF16, 1024, 1536)
        zb = norm_matmul(h, norm_mix[l], w_b, F32, 1024, 1280)
        tq = _rope_table(positions, ROT_DIM, HEAD_DIM, LANES)
        ti = _rope_table(positions, IDX_ROT_DIM, IDX_DIM, LANES)
        tk = _rope_table(positions, IDX_ROT_DIM, IDX_DIM, IDX_DIM)
        tq_rows = 512
        q_r, qi_r, k_r, ki_r, wi_r = rope_split(zb, tq, ti, tk, b, s, tq_rows)
        topk = min(TOPK_MAX, s // 4)
        tkc = 512
        wi_t = wi_r.reshape(b, s // Q_BLOCK, Q_BLOCK, IDX_HEADS).transpose(0, 1, 3, 2)
        bias = index_select(qi_r, ki_r, wi_t, b, s, topk, tkc)
        y_attn = masked_attention(q_r, k_r, za, 5 * d // HEAD_DIM, bias, b, s, tq_rows, tkc)
        h = merge_outproj(za, y_attn, conv_w[l], w_out[l].astype(BF16), h, s, 256)
        kv_w = jnp.concatenate([wk_x[l], wv_x[l]], axis=1).astype(BF16)
        kv = norm_matmul(mem.reshape(-1, d), norm_mem[l], kv_w, BF16, 1024, 512)
        h = cross_attention(h, norm_xattn[l], wq_x[l].astype(BF16), kv, wo_x[l].astype(BF16), s, 256)
        hf, idx, gates = ffn_router(h, norm_ffn[l], w_router[l], b_router[l], 256)
        n_tiles = -(-(t * TOP_K_EXPERTS) // EXPERT_BLOCK) + N_EXPERTS
        dest, buf_tok, tile_e, n_used = _route(idx[:, :TOP_K_EXPERTS], t, n_tiles)
        xs = hf[buf_tok]
        y = expert_ffn(xs, tile_e, n_used, w_gate_up[l], b_gate_up[l], w_down[l], b_down[l], 1024, 2048)
        dest2 = dest.reshape(t, TOP_K_EXPERTS)
        ys = [y[dest2[:, kk]] for kk in range(TOP_K_EXPERTS)]
        h = combine_norm(h, ys, gates, norm_final, 256, final=(l + 1 == w_in.shape[0]))
    return h.reshape(b, s, d)
```
